```python
import jax, jax.numpy as jnp
from jax import lax
import numpy as np

D_MODEL = 2048
BATCH = 4
SEQ = 2048
DEPTH = 4
DEC_BATCH = 8
DEC_SEQ = 8
PAST_LEN = 16384
PAGE_SIZE = 128

N_MIXERS = 2
N_FOX_LAYERS = (DEPTH + 1) // 2
N_RET_LAYERS = DEPTH // 2
FOX_HEADS = 16
FOX_HEAD_DIM = D_MODEL // FOX_HEADS
FOX_Q_BLOCK = 128
FORGET_BIAS_OFFSET = 3.0
RET_HEADS = 8
RET_QK_DIM = D_MODEL // RET_HEADS
RET_V_DIM = 2 * D_MODEL // RET_HEADS
RET_QK_WIDTH = RET_HEADS * RET_QK_DIM
RET_V_WIDTH = RET_HEADS * RET_V_DIM
RET_CHUNK = 128
ROPE_BASE = 10000.0
D_FF = 5632
NORM_EPS = 1e-6
GN_EPS = 1e-5

kernel_name = 'hybrid_fox_retnet_macaron_step'


def rms_norm(x, g):
    x32 = x.astype(jnp.float32)
    y = x32 * lax.rsqrt(jnp.mean(x32 * x32, axis=-1, keepdims=True) + NORM_EPS)
    return (y * g.astype(jnp.float32)).astype(x.dtype)


def macaron_half(x, g, w_gate, w_up, w_down):
    h = rms_norm(x, g)
    return x + 0.5 * ((jax.nn.silu(h @ w_gate) * (h @ w_up)) @ w_down)


def gather_pages(pool, page_table):
    b, n_pages = page_table.shape
    g = pool[page_table]
    return g.reshape((b, n_pages * PAGE_SIZE) + pool.shape[2:])


def fox_project(h, w_in, b_f):
    b, t, _ = h.shape
    proj = h @ w_in
    q, k, v, f = jnp.split(proj, [D_MODEL, 2 * D_MODEL, 3 * D_MODEL], axis=-1)
    shape = (b, t, FOX_HEADS, FOX_HEAD_DIM)
    logf = jax.nn.log_sigmoid((f + b_f).astype(jnp.float32))
    return q.reshape(shape), k.reshape(shape), v.reshape(shape), logf


def fox_attend_block(q, cq, qpos, k, v, ck, kpos):
    s = jnp.einsum('bqhd,bkhd->bhqk', q, k).astype(jnp.float32) * (FOX_HEAD_DIM ** -0.5)
    decay = jnp.transpose(cq, (0, 2, 1))[:, :, :, None] - jnp.transpose(ck, (0, 2, 1))[:, :, None, :]
    mask = kpos[None, :] <= qpos[:, None]
    s = jnp.where(mask, s + decay, -jnp.inf)
    p = jax.nn.softmax(s, axis=-1)
    return jnp.einsum('bhqk,bkhd->bqhd', p.astype(v.dtype), v)


def fox_prompt(h, w_in, b_f, w_out):
    b, t, _ = h.shape
    q, k, v, logf = fox_project(h, w_in, b_f)
    c = jnp.cumsum(logf, axis=1)
    pos = jnp.arange(t)
    nb = t // FOX_Q_BLOCK
    qb = q.reshape(b, nb, FOX_Q_BLOCK, FOX_HEADS, FOX_HEAD_DIM).transpose(1, 0, 2, 3, 4)
    cqb = c.reshape(b, nb, FOX_Q_BLOCK, FOX_HEADS).transpose(1, 0, 2, 3)
    pb = pos.reshape(nb, FOX_Q_BLOCK)
    ob = lax.map(lambda a: fox_attend_block(a[0], a[1], a[2], k, v, c, pos), (qb, cqb, pb))
    o = ob.transpose(1, 0, 2, 3, 4).reshape(b, t, D_MODEL)
    return o @ w_out, k, v, logf


def fox_sample(h, cache_k, cache_v, cache_logf, page_table, w_in, b_f, w_out):
    b, t, _ = h.shape
    q, k, v, logf = fox_project(h, w_in, b_f)
    past = page_table.shape[1] * PAGE_SIZE
    k_all = jnp.concatenate([gather_pages(cache_k, page_table).astype(k.dtype), k], axis=1)
    v_all = jnp.concatenate([gather_pages(cache_v, page_table).astype(v.dtype), v], axis=1)
    logf_all = jnp.concatenate([gather_pages(cache_logf, page_table).astype(jnp.float32), logf], axis=1)
    c = jnp.cumsum(logf_all, axis=1)
    kpos = jnp.arange(past + t)
    qpos = past + jnp.arange(t)
    o = fox_attend_block(q, c[:, past:], qpos, k_all, v_all, c, kpos)
    return o.reshape(b, t, D_MODEL) @ w_out, k, v, logf


def rope(x, pos):
    half = x.shape[-1] // 2
    inv = ROPE_BASE ** (-jnp.arange(half, dtype=jnp.float32) / half)
    ang = pos.astype(jnp.float32)[:, None] * inv[None, :]
    cos = jnp.cos(ang)[None, :, None, :]
    sin = jnp.sin(ang)[None, :, None, :]
    x1 = x[..., :half].astype(jnp.float32)
    x2 = x[..., half:].astype(jnp.float32)
    return jnp.concatenate([x1 * cos - x2 * sin, x1 * sin + x2 * cos], axis=-1).astype(x.dtype)


def ret_log_gamma():
    return jnp.log1p(-jnp.exp2(-5.0 - jnp.arange(RET_HEADS, dtype=jnp.float32)))


def ret_project(h, w_in, pos):
    b, t, _ = h.shape
    proj = h @ w_in
    q, k, v, g = jnp.split(proj, [RET_QK_WIDTH, 2 * RET_QK_WIDTH, 2 * RET_QK_WIDTH + RET_V_WIDTH], axis=-1)
    q = rope(q.reshape(b, t, RET_HEADS, RET_QK_DIM), pos)
    k = rope(k.reshape(b, t, RET_HEADS, RET_QK_DIM), pos) * (RET_QK_DIM ** -0.5)
    v = v.reshape(b, t, RET_HEADS, RET_V_DIM)
    return q, k, v, g


def retention_chunk(q, k, v, state):
    t = q.shape[1]
    lg = ret_log_gamma()
    n = jnp.arange(t, dtype=jnp.float32)
    diff = n[:, None] - n[None, :]
    causal = diff >= 0
    d_intra = jnp.where(causal[None], jnp.exp(jnp.where(causal, diff, 0.0)[None] * lg[:, None, None]), 0.0)
    qf = q.astype(jnp.float32)
    kf = k.astype(jnp.float32)
    vf = v.astype(jnp.float32)
    s = jnp.einsum('bqhd,bkhd->bhqk', qf, kf) * d_intra[None]
    intra = jnp.einsum('bhqk,bkhe->bqhe', s, vf)
    cross = jnp.einsum('bqhd,bhde->bqhe', qf, state) * jnp.exp((n[:, None] + 1.0) * lg[None, :])[None, :, :, None]
    k_dec = kf * jnp.exp((t - 1.0 - n)[:, None] * lg[None, :])[None, :, :, None]
    new_state = jnp.exp(t * lg)[None, :, None, None] * state + jnp.einsum('bkhd,bkhe->bhde', k_dec, vf)
    return intra + cross, new_state


def ret_output(y, g, w_out):
    b, t = y.shape[:2]
    mu = jnp.mean(y, axis=-1, keepdims=True)
    var = jnp.mean(jnp.square(y - mu), axis=-1, keepdims=True)
    y = (y - mu) * lax.rsqrt(var + GN_EPS)
    y = y.reshape(b, t, RET_V_WIDTH).astype(g.dtype) * jax.nn.silu(g)
    return y @ w_out


def ret_prompt(h, w_in, w_out):
    b, t, _ = h.shape
    q, k, v, g = ret_project(h, w_in, jnp.arange(t))
    nc = t // RET_CHUNK

    def chunks(a):
        return a.reshape((b, nc, RET_CHUNK) + a.shape[2:]).swapaxes(0, 1)

    def step(st, xs):
        o, st = retention_chunk(xs[0], xs[1], xs[2], st)
        return st, o

    state0 = jnp.zeros((b, RET_HEADS, RET_QK_DIM, RET_V_DIM), jnp.float32)
    st, ob = lax.scan(step, state0, (chunks(q), chunks(k), chunks(v)))
    y = ob.swapaxes(0, 1).reshape(b, t, RET_HEADS, RET_V_DIM)
    return ret_output(y, g, w_out), st


def ret_sample(h, state, w_in, w_out):
    t = h.shape[1]
    q, k, v, g = ret_project(h, w_in, PAST_LEN + jnp.arange(t))
    y, st = retention_chunk(q, k, v, state.astype(jnp.float32))
    return ret_output(y, g, w_out), st


def setup_inputs(seed: int = 0) -> dict:
    key = jax.random.key(seed)
    ks = jax.random.split(key, 32)
    n_pages = PAST_LEN // PAGE_SIZE
    n_used = DEC_BATCH * n_pages
    n_pool = (5 * n_used) // 4

    def nrm(k, shape, scale):
        return jax.random.normal(k, shape, jnp.float32) * scale

    def gain(k, shape):
        return 1.0 + 0.01 * jax.random.normal(k, shape, jnp.float32)

    fox_in_w = 3 * D_MODEL + FOX_HEADS
    ret_in_w = 2 * RET_QK_WIDTH + 2 * RET_V_WIDTH
    return {
        'x_prompt': nrm(ks[0], (BATCH, SEQ, D_MODEL), 1.0),
        'x_sample': nrm(ks[1], (DEC_BATCH, DEC_SEQ, D_MODEL), 1.0),
        'cache_fox_k': nrm(ks[2], (N_FOX_LAYERS, n_pool, PAGE_SIZE, FOX_HEADS, FOX_HEAD_DIM), 1.0),
        'cache_fox_v': nrm(ks[3], (N_FOX_LAYERS, n_pool, PAGE_SIZE, FOX_HEADS, FOX_HEAD_DIM), 1.0),
        'cache_fox_logf': jax.nn.log_sigmoid(FORGET_BIAS_OFFSET + nrm(ks[4], (N_FOX_LAYERS, n_pool, PAGE_SIZE, FOX_HEADS), 1.0)),
        'state_ret': nrm(ks[5], (N_RET_LAYERS, DEC_BATCH, RET_HEADS, RET_QK_DIM, RET_V_DIM), 1.0),
        'page_table': jax.random.permutation(ks[6], n_pool)[:n_used].reshape(DEC_BATCH, n_pages).astype(jnp.int32),
        'norm_ffn1': gain(ks[7], (DEPTH, D_MODEL)),
        'ffn1_w_gate': nrm(ks[8], (DEPTH, D_MODEL, D_FF), D_MODEL ** -0.5),
        'ffn1_w_up': nrm(ks[9], (DEPTH, D_MODEL, D_FF), D_MODEL ** -0.5),
        'ffn1_w_down': nrm(ks[10], (DEPTH, D_FF, D_MODEL), D_FF ** -0.5),
        'norm_mix': gain(ks[11], (DEPTH, D_MODEL)),
        'fox_w_in': nrm(ks[12], (N_FOX_LAYERS, D_MODEL, fox_in_w), D_MODEL ** -0.5),
        'fox_b_f': FORGET_BIAS_OFFSET + 0.1 * jax.random.normal(ks[13], (N_FOX_LAYERS, FOX_HEADS), jnp.float32),
        'fox_w_out': nrm(ks[14], (N_FOX_LAYERS, D_MODEL, D_MODEL), D_MODEL ** -0.5),
        'ret_w_in': nrm(ks[15], (N_RET_LAYERS, D_MODEL, ret_in_w), D_MODEL ** -0.5),
        'ret_w_out': nrm(ks[16], (N_RET_LAYERS, RET_V_WIDTH, D_MODEL), RET_V_WIDTH ** -0.5),
        'norm_ffn2': gain(ks[17], (DEPTH, D_MODEL)),
        'ffn2_w_gate': nrm(ks[18], (DEPTH, D_MODEL, D_FF), D_MODEL ** -0.5),
        'ffn2_w_up': nrm(ks[19], (DEPTH, D_MODEL, D_FF), D_MODEL ** -0.5),
        'ffn2_w_down': nrm(ks[20], (DEPTH, D_FF, D_MODEL), D_FF ** -0.5),
        'norm_final': gain(ks[21], (D_MODEL,)),
    }


def reference(x_prompt, x_sample, cache_fox_k, cache_fox_v, cache_fox_logf, state_ret, page_table,
              norm_ffn1, ffn1_w_gate, ffn1_w_up, ffn1_w_down, norm_mix,
              fox_w_in, fox_b_f, fox_w_out, ret_w_in, ret_w_out,
              norm_ffn2, ffn2_w_gate, ffn2_w_up, ffn2_w_down, norm_final):
    yp, ys = x_prompt, x_sample
    kp_l, vp_l, lp_l, ks_l, vs_l, ls_l, sp_l, ss_l = [], [], [], [], [], [], [], []
    for i in range(DEPTH):
        yp = macaron_half(yp, norm_ffn1[i], ffn1_w_gate[i], ffn1_w_up[i], ffn1_w_down[i])
        ys = macaron_half(ys, norm_ffn1[i], ffn1_w_gate[i], ffn1_w_up[i], ffn1_w_down[i])
        hp = rms_norm(yp, norm_mix[i])
        hs = rms_norm(ys, norm_mix[i])
        j = i // N_MIXERS
        if i % N_MIXERS == 0:
            op, kp, vp, lp = fox_prompt(hp, fox_w_in[j], fox_b_f[j], fox_w_out[j])
            os_, ks_, vs_, ls_ = fox_sample(hs, cache_fox_k[j], cache_fox_v[j], cache_fox_logf[j], page_table,
                                            fox_w_in[j], fox_b_f[j], fox_w_out[j])
            kp_l.append(kp); vp_l.append(vp); lp_l.append(lp)
            ks_l.append(ks_); vs_l.append(vs_); ls_l.append(ls_)
        else:
            op, sp = ret_prompt(hp, ret_w_in[j], ret_w_out[j])
            os_, ss = ret_sample(hs, state_ret[j], ret_w_in[j], ret_w_out[j])
            sp_l.append(sp); ss_l.append(ss)
        yp = yp + op
        ys = ys + os_
        yp = macaron_half(yp, norm_ffn2[i], ffn2_w_gate[i], ffn2_w_up[i], ffn2_w_down[i])
        ys = macaron_half(ys, norm_ffn2[i], ffn2_w_gate[i], ffn2_w_up[i], ffn2_w_down[i])
    y_prompt = rms_norm(yp, norm_final)
    y_sample = rms_norm(ys, norm_final)
    fox_k_prompt = jnp.stack(kp_l)
    fox_v_prompt = jnp.stack(vp_l)
    fox_logf_prompt = jnp.stack(lp_l)
    fox_k_sample = jnp.stack(ks_l)
    fox_v_sample = jnp.stack(vs_l)
    fox_logf_sample = jnp.stack(ls_l)
    ret_state_prompt = jnp.stack(sp_l)
    ret_state_sample = jnp.stack(ss_l)
    return (y_prompt, y_sample, fox_k_prompt, fox_v_prompt, fox_logf_prompt,
            fox_k_sample, fox_v_sample, fox_logf_sample, ret_state_prompt, ret_state_sample)
```

```python
import functools
import math

import jax
import jax.numpy as jnp
from jax import lax
from jax.experimental import pallas as pl
from jax.experimental.pallas import tpu as pltpu

F32 = jnp.float32
BF16 = jnp.bfloat16

NORM_EPS = 1e-6
GN_EPS = 1e-5
ROPE_BASE = 10000.0
NEG_BIG = -1e30

V7X_VMEM_BYTES = 64 * 1024 * 1024
V7X_LANES = 128
V7X_BF16_SUBLANES = 16
VMEM_LIMIT = (V7X_VMEM_BYTES * 7) // 8

SAMPLE_ROWS = V7X_BF16_SUBLANES
RET_CHUNK = 128


def _tile(n, target, align):
    for t in range(min(n, target), 0, -1):
        if n % t == 0 and t % align == 0:
            return t
    raise ValueError(f"no tile for {n} (target {target}, align {align})")


def _params(*sem):
    return pltpu.CompilerParams(dimension_semantics=sem, vmem_limit_bytes=VMEM_LIMIT)


def _rms_norm(x, g):
    ms = jnp.mean(x * x, axis=-1, keepdims=True)
    return (x * lax.rsqrt(ms + NORM_EPS)) * g


def _ffn_kernel(x_ref, g_ref, wgu_ref, wd_ref, o_ref, h_ref, *, tf):
    @pl.when(pl.program_id(1) == 0)
    def _():
        x = x_ref[...]
        h_ref[...] = _rms_norm(x, g_ref[...]).astype(BF16)
        o_ref[...] = x

    ab = jnp.dot(h_ref[...], wgu_ref[...], preferred_element_type=F32)
    a = ab[:, :tf]
    b = ab[:, tf:]
    p = (a * jax.nn.sigmoid(a)) * (0.5 * b)
    o_ref[...] += jnp.dot(p.astype(BF16), wd_ref[...], preferred_element_type=F32)


def _ffn(x, g, wgu, wd, *, tm, tf):
    r, d = x.shape
    f = wd.shape[0]
    return pl.pallas_call(
        functools.partial(_ffn_kernel, tf=tf),
        out_shape=jax.ShapeDtypeStruct((r, d), F32),
        grid=(r // tm, f // tf),
        in_specs=[
            pl.BlockSpec((tm, d), lambda i, j: (i, 0)),
            pl.BlockSpec((1, d), lambda i, j: (0, 0)),
            pl.BlockSpec((d, 2 * tf), lambda i, j: (0, j)),
            pl.BlockSpec((tf, d), lambda i, j: (j, 0)),
        ],
        out_specs=pl.BlockSpec((tm, d), lambda i, j: (i, 0)),
        scratch_shapes=[pltpu.VMEM((tm, d), BF16)],
        compiler_params=_params("parallel", "arbitrary"),
        name="ffn",
    )(x, g, wgu, wd)


def _pack_gate_up(w_gate, w_up, tf):
    l, d, f = w_gate.shape
    wg = w_gate.astype(BF16).reshape(l, d, f // tf, 1, tf)
    wu = w_up.astype(BF16).reshape(l, d, f // tf, 1, tf)
    return jnp.concatenate([wg, wu], axis=3).reshape(l, d, 2 * f)


def _norm_matmul_kernel(x_ref, g_ref, w_ref, o_ref, h_ref):
    @pl.when(pl.program_id(1) == 0)
    def _():
        h_ref[...] = _rms_norm(x_ref[...], g_ref[...]).astype(BF16)

    o_ref[...] = jnp.dot(h_ref[...], w_ref[...], preferred_element_type=F32).astype(o_ref.dtype)


def _norm_matmul(x, g, w, *, tm, tn, out_dtype):
    r, d = x.shape
    n = w.shape[1]
    return pl.pallas_call(
        _norm_matmul_kernel,
        out_shape=jax.ShapeDtypeStruct((r, n), out_dtype),
        grid=(r // tm, n // tn),
        in_specs=[
            pl.BlockSpec((tm, d), lambda i, j: (i, 0)),
            pl.BlockSpec((1, d), lambda i, j: (0, 0)),
            pl.BlockSpec((d, tn), lambda i, j: (0, j)),
        ],
        out_specs=pl.BlockSpec((tm, tn), lambda i, j: (i, j)),
        scratch_shapes=[pltpu.VMEM((tm, d), BF16)],
        compiler_params=_params("parallel", "arbitrary"),
        name="norm_matmul",
    )(x, g, w)


def _forget_gate_kernel(x_ref, g_ref, w_ref, b_ref, o_ref):
    h = _rms_norm(x_ref[...], g_ref[...]).astype(BF16)
    z = jnp.dot(h, w_ref[...], preferred_element_type=F32) + b_ref[...]
    o_ref[...] = -(jnp.maximum(-z, 0.0) + jnp.log1p(jnp.exp(-jnp.abs(z))))


def _forget_gate(x, g, w_f, b_f, *, tm):
    r, d = x.shape
    n = w_f.shape[1]
    return pl.pallas_call(
        _forget_gate_kernel,
        out_shape=jax.ShapeDtypeStruct((r, n), F32),
        grid=(r // tm,),
        in_specs=[
            pl.BlockSpec((tm, d), lambda i: (i, 0)),
            pl.BlockSpec((1, d), lambda i: (0, 0)),
            pl.BlockSpec((d, n), lambda i: (0, 0)),
            pl.BlockSpec((1, n), lambda i: (0, 0)),
        ],
        out_specs=pl.BlockSpec((tm, n), lambda i: (i, 0)),
        compiler_params=_params("parallel"),
        name="forget_gate",
    )(x, g, w_f, b_f)


def _matmul_res_kernel(a_ref, w_ref, r_ref, o_ref):
    o_ref[...] = r_ref[...] + jnp.dot(a_ref[...], w_ref[...], preferred_element_type=F32)


def _matmul_res(a, w, res, *, tm, tn):
    r, k = a.shape
    n = w.shape[1]
    return pl.pallas_call(
        _matmul_res_kernel,
        out_shape=jax.ShapeDtypeStruct((r, n), F32),
        grid=(r // tm, n // tn),
        in_specs=[
            pl.BlockSpec((tm, k), lambda i, j: (i, 0)),
            pl.BlockSpec((k, tn), lambda i, j: (0, j)),
            pl.BlockSpec((tm, tn), lambda i, j: (i, j)),
        ],
        out_specs=pl.BlockSpec((tm, tn), lambda i, j: (i, j)),
        compiler_params=_params("parallel", "arbitrary"),
        name="matmul_res",
    )(a, w, res)


def _final_norm_kernel(x_ref, g_ref, o_ref):
    o_ref[...] = _rms_norm(x_ref[...], g_ref[...])


def _final_norm(x, g, *, tm):
    r, d = x.shape
    return pl.pallas_call(
        _final_norm_kernel,
        out_shape=jax.ShapeDtypeStruct((r, d), F32),
        grid=(r // tm,),
        in_specs=[pl.BlockSpec((tm, d), lambda i: (i, 0)), pl.BlockSpec((1, d), lambda i: (0, 0))],
        out_specs=pl.BlockSpec((tm, d), lambda i: (i, 0)),
        compiler_params=_params("parallel"),
        name="final_norm",
    )(x, g)


def _upper_ones(n):
    row = lax.broadcasted_iota(jnp.int32, (n, n), 0)
    col = lax.broadcasted_iota(jnp.int32, (n, n), 1)
    return jnp.where(row <= col, 1.0, 0.0).astype(BF16)


def _split3(x):
    p1 = x.astype(BF16)
    r1 = x - p1.astype(F32)
    p2 = r1.astype(BF16)
    p3 = (r1 - p2.astype(F32)).astype(BF16)
    return p1, p2, p3


def _lane_cumsum_block(xt, upper):
    out = None
    for piece in _split3(xt):
        part = jnp.dot(piece, upper, preferred_element_type=F32)
        out = part if out is None else out + part
    return out


def _prompt_cumsum_kernel(lf_ref, c_ref, *, heads, blk):
    t = lf_ref.shape[0]
    upper = _upper_ones(blk)
    carry = jnp.zeros((heads, 1), F32)
    for i in range(t // blk):
        xt = lf_ref[i * blk:(i + 1) * blk, :].T[:heads, :]
        local = _lane_cumsum_block(xt, upper) + carry
        c_ref[0, :, i * blk:(i + 1) * blk] = local
        carry = local[:, blk - 1:blk]


def _prompt_cumsum(logf, *, batch, seq, heads):
    return pl.pallas_call(
        functools.partial(_prompt_cumsum_kernel, heads=heads, blk=V7X_LANES),
        out_shape=jax.ShapeDtypeStruct((batch, heads, seq), F32),
        grid=(batch,),
        in_specs=[pl.BlockSpec((seq, logf.shape[1]), lambda b: (b, 0))],
        out_specs=pl.BlockSpec((1, heads, seq), lambda b: (b, 0, 0)),
        compiler_params=_params("parallel"),
        name="prompt_cumsum",
    )(logf)


def _decode_cumsum_kernel(pt_ref, *refs, n_pg, heads, page, n_new):
    page_refs = refs[:n_pg]
    new_ref = refs[n_pg]
    c_ref, cn_ref, carry_ref = refs[n_pg + 1:]
    g = pl.program_id(1)

    @pl.when(g == 0)
    def _():
        carry_ref[...] = jnp.zeros_like(carry_ref)

    upper = _upper_ones(page)
    carry = carry_ref[...]
    for i in range(n_pg):
        x = page_refs[i][...]
        xt = jnp.concatenate([x, jnp.zeros((page, V7X_LANES - heads), F32)], axis=1).T[:heads, :]
        local = _lane_cumsum_block(xt, upper) + carry
        c_ref[0, :, i * page:(i + 1) * page] = local
        carry = local[:, page - 1:page]
    carry_ref[...] = carry

    @pl.when(g == pl.num_programs(1) - 1)
    def _():
        xn = new_ref[...]
        xn = jnp.concatenate([xn, jnp.zeros((V7X_LANES - xn.shape[0], xn.shape[1]), F32)], axis=0)
        xt = xn.T[:heads, :]
        cn_ref[0] = _lane_cumsum_block(xt, upper) + carry


def _decode_cumsum(page_table, cache_logf, layer, logf, *, n_prompt_rows, heads):
    bd, n_pages = page_table.shape
    page = cache_logf.shape[2]
    n_pg = _tile(n_pages, 16, 1)
    past = n_pages * page

    def page_spec(i):
        return pl.BlockSpec((None, None, page, heads),
                            lambda b, g, pt: (layer, pt[b, g * n_pg + i], 0, 0))

    new_block = n_prompt_rows // SAMPLE_ROWS
    grid_spec = pltpu.PrefetchScalarGridSpec(
        num_scalar_prefetch=1,
        grid=(bd, n_pages // n_pg),
        in_specs=[page_spec(i) for i in range(n_pg)]
        + [pl.BlockSpec((SAMPLE_ROWS, logf.shape[1]), lambda b, g, pt: (new_block + b, 0))],
        out_specs=[
            pl.BlockSpec((1, heads, n_pg * page), lambda b, g, pt: (b, 0, g)),
            pl.BlockSpec((1, heads, V7X_LANES), lambda b, g, pt: (b, 0, 0)),
        ],
        scratch_shapes=[pltpu.VMEM((heads, 1), F32)],
    )
    return pl.pallas_call(
        functools.partial(_decode_cumsum_kernel, n_pg=n_pg, heads=heads, page=page, n_new=None),
        out_shape=[jax.ShapeDtypeStruct((bd, heads, past), F32),
                   jax.ShapeDtypeStruct((bd, heads, V7X_LANES), F32)],
        grid_spec=grid_spec,
        compiler_params=_params("parallel", "arbitrary"),
        name="decode_cumsum",
    )(page_table, *([cache_logf] * n_pg), logf)


def _fox_prompt_kernel(q_ref, k_ref, v_ref, ck_ref, o_ref, kb_ref, vb_ref, *, tq, scale):
    qi = pl.program_id(2)

    @pl.when(qi == 0)
    def _():
        kb_ref[...] = k_ref[...].astype(BF16)
        vb_ref[...] = v_ref[...].astype(BF16)

    q = q_ref[...].astype(BF16)
    dh = q.shape[1]
    row = lax.broadcasted_iota(jnp.int32, (tq, tq), 0)
    col = lax.broadcasted_iota(jnp.int32, (tq, tq), 1)
    causal = col <= row

    def body(kj, carry):
        m, l, acc = carry
        off = pl.multiple_of(kj * tq, tq)
        kblk = kb_ref[pl.ds(off, tq), :]
        vblk = vb_ref[pl.ds(off, tq), :]
        s = lax.dot_general(q, kblk, (((1,), (1,)), ((), ())), preferred_element_type=F32) * scale
        s = s - ck_ref[0, pl.ds(kj, 1), :]
        s = jnp.where(jnp.logical_or(kj < qi, causal), s, NEG_BIG)
        m_new = jnp.maximum(m, jnp.max(s, axis=-1, keepdims=True))
        alpha = jnp.exp(m - m_new)
        p = jnp.exp(s - m_new)
        l = alpha * l + jnp.sum(p, axis=-1, keepdims=True)
        acc = alpha * acc + jnp.dot(p.astype(BF16), vblk, preferred_element_type=F32)
        return m_new, l, acc

    init = (jnp.full((tq, 1), NEG_BIG, F32), jnp.zeros((tq, 1), F32), jnp.zeros((tq, dh), F32))
    _, l, acc = lax.fori_loop(0, qi + 1, body, init)
    o_ref[...] = (acc / l).astype(o_ref.dtype)


def _fox_prompt(proj, ck, *, rows, batch, seq, heads, dh, tq):
    nq = seq // tq
    return pl.pallas_call(
        functools.partial(_fox_prompt_kernel, tq=tq, scale=dh ** -0.5),
        out_shape=jax.ShapeDtypeStruct((rows, heads * dh), F32),
        grid=(batch, heads, nq),
        in_specs=[
            pl.BlockSpec((tq, dh), lambda b, h, i: (b * nq + i, h)),
            pl.BlockSpec((seq, dh), lambda b, h, i: (b, heads + h)),
            pl.BlockSpec((seq, dh), lambda b, h, i: (b, 2 * heads + h)),
            pl.BlockSpec((1, nq, tq), lambda b, h, i: (b * heads + h, 0, 0)),
        ],
        out_specs=pl.BlockSpec((tq, dh), lambda b, h, i: (b * nq + i, h)),
        scratch_shapes=[pltpu.VMEM((seq, dh), BF16), pltpu.VMEM((seq, dh), BF16)],
        compiler_params=_params("parallel", "parallel", "arbitrary"),
        name="fox_prompt",
    )(proj, proj, proj, ck)


def _fox_decode_kernel(pt_ref, q_ref, kn_ref, vn_ref, ck_ref, cn_ref, *refs,
                       n_pg, heads, dh, page, n_new, scale):
    k_refs = refs[:n_pg]
    v_refs = refs[n_pg:2 * n_pg]
    o_ref, m_ref, l_ref, acc_ref = refs[2 * n_pg + 1:]
    g = pl.program_id(1)
    nt = q_ref.shape[0]

    @pl.when(g == 0)
    def _():
        m_ref[...] = jnp.full_like(m_ref, NEG_BIG)
        l_ref[...] = jnp.zeros_like(l_ref)
        acc_ref[...] = jnp.zeros_like(acc_ref)

    def update(h, s, v_blocks):
        m_old = m_ref[h]
        m_new = jnp.maximum(m_old, jnp.max(s, axis=-1, keepdims=True))
        alpha = jnp.exp(m_old - m_new)
        p = jnp.exp(s - m_new)
        l_ref[h] = alpha * l_ref[h] + jnp.sum(p, axis=-1, keepdims=True)
        pv = None
        for i, vb in enumerate(v_blocks):
            w = p[:, i * page:(i + 1) * page].astype(BF16)
            part = jnp.dot(w, vb, preferred_element_type=F32)
            pv = part if pv is None else pv + part
        acc_ref[h] = alpha * acc_ref[h] + pv
        m_ref[h] = m_new

    for h in range(heads):
        q_h = q_ref[:, h * dh:(h + 1) * dh].astype(BF16)
        s_parts = []
        for i in range(n_pg):
            k_h = k_refs[i][:, h, :].astype(BF16)
            s_parts.append(lax.dot_general(q_h, k_h, (((1,), (1,)), ((), ())),
                                           preferred_element_type=F32))
        s = jnp.concatenate(s_parts, axis=1) * scale - ck_ref[0, h:h + 1, :]
        update(h, s, [v_refs[i][:, h, :].astype(BF16) for i in range(n_pg)])

    @pl.when(g == pl.num_programs(1) - 1)
    def _():
        row = lax.broadcasted_iota(jnp.int32, (nt, page), 0)
        col = lax.broadcasted_iota(jnp.int32, (nt, page), 1)
        visible = jnp.logical_and(col <= row, col < n_new)
        pad = jnp.zeros((page - nt, dh), F32)
        for h in range(heads):
            q_h = q_ref[:, h * dh:(h + 1) * dh].astype(BF16)
            k_h = jnp.concatenate([kn_ref[:, h * dh:(h + 1) * dh], pad], axis=0).astype(BF16)
            v_h = jnp.concatenate([vn_ref[:, h * dh:(h + 1) * dh], pad], axis=0).astype(BF16)
            s = lax.dot_general(q_h, k_h, (((1,), (1,)), ((), ())), preferred_element_type=F32)
            s = s * scale - cn_ref[0, h:h + 1, :]
            s = jnp.where(visible, s, NEG_BIG)
            update(h, s, [v_h])
            o_ref[:, h * dh:(h + 1) * dh] = (acc_ref[h] / l_ref[h]).astype(o_ref.dtype)


def _fox_decode(page_table, proj, ck_past, c_new, cache_k, cache_v, layer, attn, *,
                n_prompt_rows, heads, dh, n_new, n_pg):
    bd, n_pages = page_table.shape
    page = cache_k.shape[2]
    hd = heads * dh
    sblk = n_prompt_rows // SAMPLE_ROWS

    def row_spec(col):
        return pl.BlockSpec((SAMPLE_ROWS, hd), lambda b, g, pt: (sblk + b, col))

    def page_spec(i):
        return pl.BlockSpec((None, None, page, heads, dh),
                            lambda b, g, pt: (layer, pt[b, g * n_pg + i], 0, 0, 0))

    n_in = 5 + 2 * n_pg
    grid_spec = pltpu.PrefetchScalarGridSpec(
        num_scalar_prefetch=1,
        grid=(bd, n_pages // n_pg),
        in_specs=[row_spec(0), row_spec(1), row_spec(2),
                  pl.BlockSpec((1, heads, n_pg * page), lambda b, g, pt: (b, 0, g)),
                  pl.BlockSpec((1, heads, V7X_LANES), lambda b, g, pt: (b, 0, 0))]
        + [page_spec(i) for i in range(n_pg)] * 2
        + [pl.BlockSpec(memory_space=pl.ANY)],
        out_specs=pl.BlockSpec((SAMPLE_ROWS, hd), lambda b, g, pt: (sblk + b, 0)),
        scratch_shapes=[pltpu.VMEM((heads, SAMPLE_ROWS, 1), F32),
                        pltpu.VMEM((heads, SAMPLE_ROWS, 1), F32),
                        pltpu.VMEM((heads, SAMPLE_ROWS, dh), F32)],
    )
    return pl.pallas_call(
        functools.partial(_fox_decode_kernel, n_pg=n_pg, heads=heads, dh=dh, page=page,
                          n_new=n_new, scale=dh ** -0.5),
        out_shape=jax.ShapeDtypeStruct(attn.shape, attn.dtype),
        grid_spec=grid_spec,
        input_output_aliases={n_in + 1: 0},
        compiler_params=_params("parallel", "arbitrary"),
        name="fox_decode",
    )(page_table, proj, proj, proj, ck_past, c_new,
      *([cache_k] * n_pg), *([cache_v] * n_pg), attn)


def _ret_tables(heads, chunk, t_real):
    lg = jnp.log1p(-jnp.exp2(-5.0 - jnp.arange(heads, dtype=F32)))
    n = jnp.arange(chunk, dtype=F32)
    diff = n[:, None] - n[None, :]
    causal = diff >= 0
    dmat = jnp.where(causal[None], jnp.exp(jnp.where(causal, diff, 0.0)[None] * lg[:, None, None]), 0.0)
    cross = jnp.exp((n[None, :] + 1.0) * lg[:, None])
    kdec = jnp.where(n[None, :] < t_real, jnp.exp((t_real - 1.0 - n)[None, :] * lg[:, None]), 0.0)
    full = jnp.broadcast_to(jnp.exp(t_real * lg)[:, None], (heads, chunk))
    dvec = jnp.stack([cross, kdec, full], axis=-1)
    return dmat.astype(F32), dvec.astype(F32)


def _rope_tables(positions, dk):
    half = dk // 2
    inv = ROPE_BASE ** (-jnp.arange(half, dtype=F32) / half)
    ang = positions.astype(F32)[:, None] * inv[None, :]
    return jnp.cos(ang), jnp.sin(ang)


def _ret_kernel(*refs, has_state, chunk, dk):
    if has_state:
        q_ref, k_ref, v_ref, g_ref, cos_ref, sin_ref, dmat_ref, dvec_ref, s0_ref, y_ref, st_ref = refs
    else:
        q_ref, k_ref, v_ref, g_ref, cos_ref, sin_ref, dmat_ref, dvec_ref, y_ref, st_ref = refs
        s0_ref = None
    rows = q_ref.shape[0]
    half = dk // 2

    @pl.when(pl.program_id(2) == 0)
    def _():
        if has_state:
            st_ref[...] = s0_ref[...]
        else:
            st_ref[...] = jnp.zeros_like(st_ref)

    def padded(x):
        x = x.astype(F32)
        if rows == chunk:
            return x
        return jnp.concatenate([x, jnp.zeros((chunk - rows, x.shape[1]), F32)], axis=0)

    cos = padded(cos_ref[...])
    sin = padded(sin_ref[...])

    def rope(x):
        x1 = x[:, :half]
        x2 = x[:, half:]
        return jnp.concatenate([x1 * cos - x2 * sin, x1 * sin + x2 * cos], axis=1)

    q = rope(padded(q_ref[...]))
    k = rope(padded(k_ref[...])) * (dk ** -0.5)
    v = padded(v_ref[...]).astype(BF16)
    dvec = dvec_ref[0]
    cross_d = dvec[:, 0:1]
    k_d = dvec[:, 1:2]
    full_d = dvec[0:1, 2:3]

    qb = q.astype(BF16)
    kb = k.astype(BF16)
    s = lax.dot_general(qb, kb, (((1,), (1,)), ((), ())), preferred_element_type=F32) * dmat_ref[0]
    state = st_ref[0, 0]
    y = jnp.dot(s.astype(BF16), v, preferred_element_type=F32)
    y = y + jnp.dot(qb, state.astype(BF16), preferred_element_type=F32) * cross_d
    kdec_t = (k * k_d).T.astype(BF16)
    st_ref[0, 0] = full_d * state + jnp.dot(kdec_t, v, preferred_element_type=F32)

    mu = jnp.mean(y, axis=-1, keepdims=True)
    yc = y - mu
    var = jnp.mean(yc * yc, axis=-1, keepdims=True)
    yn = yc * lax.rsqrt(var + GN_EPS)
    gate = padded(g_ref[...])
    out = yn * (gate * jax.nn.sigmoid(gate))
    y_ref[...] = out[:rows].astype(y_ref.dtype)


def _retention(proj, cos, sin, state0, y_prev, *, row0, batch, n_chunks, rows, t_real,
               heads, dk, dv):
    chunk = max(rows, RET_CHUNK)
    dmat, dvec = _ret_tables(heads, chunk, t_real)
    r0 = row0 // rows
    qk_w = heads * dk
    has_state = state0 is not None

    def tok_spec(width, col0):
        cb = col0 // width
        return pl.BlockSpec((rows, width), lambda b, h, c: (r0 + b * n_chunks + c, cb + h))

    in_specs = [
        tok_spec(dk, 0), tok_spec(dk, qk_w), tok_spec(dv, 2 * qk_w), tok_spec(dv, 2 * qk_w + heads * dv),
        pl.BlockSpec((rows, dk // 2), lambda b, h, c: (c, 0)),
        pl.BlockSpec((rows, dk // 2), lambda b, h, c: (c, 0)),
        pl.BlockSpec((1, chunk, chunk), lambda b, h, c: (h, 0, 0)),
        pl.BlockSpec((1, chunk, 3), lambda b, h, c: (h, 0, 0)),
    ]
    args = [proj, proj, proj, proj, cos, sin, dmat, dvec]
    if has_state:
        in_specs.append(pl.BlockSpec((1, 1, dk, dv), lambda b, h, c: (b, h, 0, 0)))
        args.append(state0)
    in_specs.append(pl.BlockSpec(memory_space=pl.ANY))
    args.append(y_prev)
    n_in = len(args)

    def kernel(*refs):
        refs = refs[:n_in - 1] + refs[n_in:]
        _ret_kernel(*refs, has_state=has_state, chunk=chunk, dk=dk)

    return pl.pallas_call(
        kernel,
        out_shape=[jax.ShapeDtypeStruct(y_prev.shape, y_prev.dtype),
                   jax.ShapeDtypeStruct((batch, heads, dk, dv), F32)],
        grid=(batch, heads, n_chunks),
        in_specs=in_specs,
        out_specs=[
            pl.BlockSpec((rows, dv), lambda b, h, c: (r0 + b * n_chunks + c, h)),
            pl.BlockSpec((1, 1, dk, dv), lambda b, h, c: (b, h, 0, 0)),
        ],
        input_output_aliases={n_in - 1: 0},
        compiler_params=_params("parallel", "parallel", "arbitrary"),
        name="retention",
    )(*args)


def kernel(x_prompt, x_sample, cache_fox_k, cache_fox_v, cache_fox_logf, state_ret, page_table, norm_ffn1, ffn1_w_gate, ffn1_w_up, ffn1_w_down, norm_mix, fox_w_in, fox_b_f, fox_w_out, ret_w_in, ret_w_out, norm_ffn2, ffn2_w_gate, ffn2_w_up, ffn2_w_down, norm_final):
    batch, seq, d = x_prompt.shape
    bd, n_new, _ = x_sample.shape
    depth = norm_ffn1.shape[0]
    f = ffn1_w_gate.shape[2]
    page, fox_heads, dh = cache_fox_k.shape[2:]
    ret_heads, dk, dv = state_ret.shape[2:]
    n_pages = page_table.shape[1]
    past = n_pages * page
    assert n_new <= SAMPLE_ROWS and fox_heads * dh == d and ret_heads * dk == d

    n_p = batch * seq
    rows = n_p + bd * SAMPLE_ROWS
    tm = _tile(rows, 1100, V7X_BF16_SUBLANES)
    tf = _tile(f, 256, V7X_LANES)
    tn = _tile(d, 512, V7X_LANES)
    tq = _tile(seq, 512, V7X_LANES)
    n_pg = _tile(n_pages, 4, 1)

    xs = jnp.pad(x_sample, ((0, 0), (0, SAMPLE_ROWS - n_new), (0, 0)))
    x = jnp.concatenate([x_prompt.reshape(n_p, d), xs.reshape(bd * SAMPLE_ROWS, d)], axis=0)

    wgu1 = _pack_gate_up(ffn1_w_gate, ffn1_w_up, tf)
    wgu2 = _pack_gate_up(ffn2_w_gate, ffn2_w_up, tf)
    wd1 = ffn1_w_down.astype(BF16)
    wd2 = ffn2_w_down.astype(BF16)
    fox_qkv_w = fox_w_in[:, :, :3 * d].astype(BF16)
    fox_f_w = jnp.pad(fox_w_in[:, :, 3 * d:], ((0, 0), (0, 0), (0, V7X_LANES - fox_heads))).astype(BF16)
    fox_f_b = jnp.pad(fox_b_f, ((0, 0), (0, V7X_LANES - fox_heads)))
    fox_out_w = fox_w_out.astype(BF16)
    ret_in_w = ret_w_in.astype(BF16)
    ret_out_w = ret_w_out.astype(BF16)

    cos_p, sin_p = _rope_tables(jnp.arange(seq), dk)
    cos_s, sin_s = _rope_tables(past + jnp.arange(SAMPLE_ROWS), dk)

    def split_rows(a):
        w = a.shape[1]
        return (a[:n_p].reshape(batch, seq, w),
                a[n_p:].reshape(bd, SAMPLE_ROWS, w)[:, :n_new])

    kp_l, vp_l, lp_l, ks_l, vs_l, ls_l, sp_l, ss_l = [], [], [], [], [], [], [], []
    for i in range(depth):
        x = _ffn(x, norm_ffn1[i][None], wgu1[i], wd1[i], tm=tm, tf=tf)
        j = i // 2
        g_mix = norm_mix[i][None]
        if i % 2 == 0:
            proj = _norm_matmul(x, g_mix, fox_qkv_w[j], tm=tm, tn=tn, out_dtype=F32)
            logf = _forget_gate(x, g_mix, fox_f_w[j], fox_f_b[j][None], tm=tm)
            c_prompt = _prompt_cumsum(logf, batch=batch, seq=seq, heads=fox_heads)
            ck = c_prompt.reshape(batch * fox_heads, seq // tq, tq)
            attn = _fox_prompt(proj, ck, rows=rows, batch=batch, seq=seq, heads=fox_heads, dh=dh, tq=tq)
            c_past, c_new = _decode_cumsum(page_table, cache_fox_logf, j, logf,
                                           n_prompt_rows=n_p, heads=fox_heads)
            attn = _fox_decode(page_table, proj, c_past, c_new, cache_fox_k, cache_fox_v, j, attn,
                               n_prompt_rows=n_p, heads=fox_heads, dh=dh, n_new=n_new, n_pg=n_pg)
            x = _matmul_res(attn.astype(BF16), fox_out_w[j], x, tm=tm, tn=tn)
            kp, ks = split_rows(proj[:, d:2 * d])
            vp, vs = split_rows(proj[:, 2 * d:3 * d])
            lp, ls = split_rows(logf[:, :fox_heads])
            kp_l.append(kp.reshape(batch, seq, fox_heads, dh))
            vp_l.append(vp.reshape(batch, seq, fox_heads, dh))
            ks_l.append(ks.reshape(bd, n_new, fox_heads, dh))
            vs_l.append(vs.reshape(bd, n_new, fox_heads, dh))
            lp_l.append(lp)
            ls_l.append(ls)
        else:
            proj = _norm_matmul(x, g_mix, ret_in_w[j], tm=tm, tn=tn, out_dtype=BF16)
            y = jnp.zeros((rows, ret_heads * dv), BF16)
            y, sp = _retention(proj, cos_p, sin_p, None, y, row0=0, batch=batch,
                               n_chunks=seq // RET_CHUNK, rows=RET_CHUNK, t_real=RET_CHUNK,
                               heads=ret_heads, dk=dk, dv=dv)
            y, ss = _retention(proj, cos_s, sin_s, state_ret[j], y, row0=n_p, batch=bd,
                               n_chunks=1, rows=SAMPLE_ROWS, t_real=n_new,
                               heads=ret_heads, dk=dk, dv=dv)
            x = _matmul_res(y, ret_out_w[j], x, tm=tm, tn=tn)
            sp_l.append(sp)
            ss_l.append(ss)
        x = _ffn(x, norm_ffn2[i][None], wgu2[i], wd2[i], tm=tm, tf=tf)

    y_all = _final_norm(x, norm_final[None], tm=tm)
    y_prompt, y_sample = split_rows(y_all)
    return (y_prompt, y_sample,
            jnp.stack(kp_l), jnp.stack(vp_l), jnp.stack(lp_l),
            jnp.stack(ks_l), jnp.stack(vs_l), jnp.stack(ls_l),
            jnp.stack(sp_l), jnp.stack(ss_l))
```

```python
import functools
import math

import jax
import jax.numpy as jnp
from jax import lax
from jax.experimental import pallas as pl
from jax.experimental.pallas import tpu as pltpu

F32 = jnp.float32
BF16 = jnp.bfloat16

NORM_EPS = 1e-6
GN_EPS = 1e-5
ROPE_BASE = 10000.0
NEG_BIG = -1e30
LOG2E = math.log2(math.e)

V7X_VMEM_BYTES = 64 * 1024 * 1024
V7X_LANES = 128
V7X_BF16_SUBLANES = 16
V7X_MXU_DIM = 256
VMEM_LIMIT = (V7X_VMEM_BYTES * 7) // 8

SAMPLE_ROWS = V7X_BF16_SUBLANES
RET_CHUNK = 128
N_DECAY_PIECES = 3
HEAD_GROUPS = 4
FOX_HEADS_PER_STEP = 2


def _tile(n, target, align):
    for t in range(min(n, target), 0, -1):
        if n % t == 0 and t % align == 0:
            return t
    raise ValueError(f"no tile for {n} (target {target}, align {align})")


def _params(*sem):
    return pltpu.CompilerParams(dimension_semantics=sem, vmem_limit_bytes=VMEM_LIMIT)


def _rms_norm(x, g):
    ms = jnp.mean(x * x, axis=-1, keepdims=True)
    return (x * lax.rsqrt(ms + NORM_EPS)) * g


def _nt_dot(a, b):
    return lax.dot_general(a, b, (((1,), (1,)), ((), ())), preferred_element_type=F32)


def _ffn_kernel(x_ref, g_ref, wgu_ref, wd_ref, o_ref, h_ref, *, tf):
    @pl.when(pl.program_id(1) == 0)
    def _():
        x = x_ref[...]
        h_ref[...] = _rms_norm(x, g_ref[...]).astype(BF16)
        o_ref[...] = x

    ab = jnp.dot(h_ref[...], wgu_ref[...], preferred_element_type=F32)
    a = ab[:, :tf]
    b = ab[:, tf:]
    p = (a * jax.nn.sigmoid(a)) * (0.5 * b)
    o_ref[...] += jnp.dot(p.astype(BF16), wd_ref[...], preferred_element_type=F32)


def _ffn(x, g, wgu, wd, *, tm, tf):
    r, d = x.shape
    f = wd.shape[0]
    return pl.pallas_call(
        functools.partial(_ffn_kernel, tf=tf),
        out_shape=jax.ShapeDtypeStruct((r, d), F32),
        grid=(r // tm, f // tf),
        in_specs=[
            pl.BlockSpec((tm, d), lambda i, j: (i, 0)),
            pl.BlockSpec((1, d), lambda i, j: (0, 0)),
            pl.BlockSpec((d, 2 * tf), lambda i, j: (0, j)),
            pl.BlockSpec((tf, d), lambda i, j: (j, 0)),
        ],
        out_specs=pl.BlockSpec((tm, d), lambda i, j: (i, 0)),
        scratch_shapes=[pltpu.VMEM((tm, d), BF16)],
        compiler_params=_params("parallel", "arbitrary"),
        name="ffn",
    )(x, g, wgu, wd)


def _ffn_weights_kernel(wg_ref, wu_ref, wd_ref, wgu_out, wd_out, *, tf):
    for s in range(wg_ref.shape[1] // tf):
        src = slice(s * tf, (s + 1) * tf)
        wgu_out[:, 2 * s * tf:(2 * s + 1) * tf] = wg_ref[:, src].astype(BF16)
        wgu_out[:, (2 * s + 1) * tf:(2 * s + 2) * tf] = wu_ref[:, src].astype(BF16)
    wd_out[...] = wd_ref[...].astype(BF16)


def _ffn_weights(w_gate, w_up, w_down, layer, *, tf):
    _, d, f = w_gate.shape
    tc = _tile(f, 2 * tf, tf)
    return pl.pallas_call(
        functools.partial(_ffn_weights_kernel, tf=tf),
        out_shape=[jax.ShapeDtypeStruct((d, 2 * f), BF16), jax.ShapeDtypeStruct((f, d), BF16)],
        grid=(f // tc,),
        in_specs=[
            pl.BlockSpec((None, d, tc), lambda j: (layer, 0, j)),
            pl.BlockSpec((None, d, tc), lambda j: (layer, 0, j)),
            pl.BlockSpec((None, tc, d), lambda j: (layer, j, 0)),
        ],
        out_specs=[pl.BlockSpec((d, 2 * tc), lambda j: (0, j)), pl.BlockSpec((tc, d), lambda j: (j, 0))],
        compiler_params=_params("parallel"),
        name="ffn_weights",
    )(w_gate, w_up, w_down)


def _norm_matmul_kernel(x_ref, g_ref, w_ref, o_ref, h_ref, *, w_transposed):
    @pl.when(pl.program_id(1) == 0)
    def _():
        h_ref[...] = _rms_norm(x_ref[...], g_ref[...]).astype(BF16)

    w = w_ref[...].astype(BF16)
    if w_transposed:
        out = _nt_dot(h_ref[...], w)
    else:
        out = jnp.dot(h_ref[...], w, preferred_element_type=F32)
    o_ref[...] = out.astype(o_ref.dtype)


def _norm_matmul(x, g, w, layer, *, n, tm, tn, out_dtype, w_transposed=False):
    r, d = x.shape
    if w_transposed:
        w_spec = pl.BlockSpec((None, tn, d), lambda i, j: (layer, j, 0))
    else:
        w_spec = pl.BlockSpec((None, d, tn), lambda i, j: (layer, 0, j))
    return pl.pallas_call(
        functools.partial(_norm_matmul_kernel, w_transposed=w_transposed),
        out_shape=jax.ShapeDtypeStruct((r, n), out_dtype),
        grid=(r // tm, n // tn),
        in_specs=[
            pl.BlockSpec((tm, d), lambda i, j: (i, 0)),
            pl.BlockSpec((1, d), lambda i, j: (0, 0)),
            w_spec,
        ],
        out_specs=pl.BlockSpec((tm, tn), lambda i, j: (i, j)),
        scratch_shapes=[pltpu.VMEM((tm, d), BF16)],
        compiler_params=_params("parallel", "arbitrary"),
        name="norm_matmul",
    )(x, g, w)


def _forget_gate_kernel(x_ref, g_ref, w_ref, b_ref, o_ref, *, heads):
    h = _rms_norm(x_ref[...], g_ref[...]).astype(BF16)
    row = lax.broadcasted_iota(jnp.int32, w_ref.shape, 0)
    w = jnp.where(row < heads, w_ref[...], 0.0).astype(BF16)
    z = _nt_dot(h, w) + b_ref[...]
    o_ref[...] = -(jnp.maximum(-z, 0.0) + jnp.log1p(jnp.exp(-jnp.abs(z))))


def _forget_gate(x, g, w_t, layer, b_f, *, row0, heads, tm):
    r, d = x.shape
    n = V7X_LANES
    cb = row0 // n
    return pl.pallas_call(
        functools.partial(_forget_gate_kernel, heads=heads),
        out_shape=jax.ShapeDtypeStruct((r, n), F32),
        grid=(r // tm,),
        in_specs=[
            pl.BlockSpec((tm, d), lambda i: (i, 0)),
            pl.BlockSpec((1, d), lambda i: (0, 0)),
            pl.BlockSpec((None, n, d), lambda i: (layer, cb, 0)),
            pl.BlockSpec((1, n), lambda i: (0, 0)),
        ],
        out_specs=pl.BlockSpec((tm, n), lambda i: (i, 0)),
        compiler_params=_params("parallel"),
        name="forget_gate",
    )(x, g, w_t, b_f)


def _matmul_res_kernel(a_ref, w_ref, r_ref, o_ref):
    w = w_ref[...].astype(BF16)
    o_ref[...] = r_ref[...] + jnp.dot(a_ref[...], w, preferred_element_type=F32)


def _matmul_res(a, w, layer, res, *, tm, tn):
    r, k = a.shape
    n = w.shape[2]
    return pl.pallas_call(
        _matmul_res_kernel,
        out_shape=jax.ShapeDtypeStruct((r, n), F32),
        grid=(r // tm, n // tn),
        in_specs=[
            pl.BlockSpec((tm, k), lambda i, j: (i, 0)),
            pl.BlockSpec((None, k, tn), lambda i, j: (layer, 0, j)),
            pl.BlockSpec((tm, tn), lambda i, j: (i, j)),
        ],
        out_specs=pl.BlockSpec((tm, tn), lambda i, j: (i, j)),
        compiler_params=_params("parallel", "arbitrary"),
        name="matmul_res",
    )(a, w, res)


def _final_norm_kernel(x_ref, g_ref, o_ref):
    o_ref[...] = _rms_norm(x_ref[...], g_ref[...])


def _final_norm(x, g, *, row0, nrows, tr):
    d = x.shape[1]
    b0 = row0 // tr
    return pl.pallas_call(
        _final_norm_kernel,
        out_shape=jax.ShapeDtypeStruct((nrows, d), F32),
        grid=(nrows // tr,),
        in_specs=[pl.BlockSpec((tr, d), lambda i: (b0 + i, 0)), pl.BlockSpec((1, d), lambda i: (0, 0))],
        out_specs=pl.BlockSpec((tr, d), lambda i: (i, 0)),
        compiler_params=_params("parallel"),
        name="final_norm",
    )(x, g)


def _heads_layout_kernel(*refs, heads, dh, aliased):
    k_ref, v_ref = refs[:2]
    ko_ref, vo_ref = refs[2 + 2 * aliased:]
    tr = k_ref.shape[0]
    for src, dst in ((k_ref, ko_ref), (v_ref, vo_ref)):
        for h in range(heads):
            dst[pl.ds(h, tr, stride=heads), :] = src[:, h * dh:(h + 1) * dh]


def _heads_layout(proj, prev, layer, n_layers, *, row0, nrows, tr, heads, dh):
    d = heads * dh
    b0 = row0 // tr
    aliased = prev is not None
    shape = jax.ShapeDtypeStruct((n_layers, nrows * heads, dh), F32)
    in_specs = [pl.BlockSpec((tr, d), lambda i: (b0 + i, 1)),
                pl.BlockSpec((tr, d), lambda i: (b0 + i, 2))]
    args = [proj, proj]
    if aliased:
        in_specs += [pl.BlockSpec(memory_space=pl.ANY)] * 2
        args += list(prev)
    out_spec = pl.BlockSpec((None, tr * heads, dh), lambda i: (layer, i, 0))
    return pl.pallas_call(
        functools.partial(_heads_layout_kernel, heads=heads, dh=dh, aliased=aliased),
        out_shape=[shape, shape],
        grid=(nrows // tr,),
        in_specs=in_specs,
        out_specs=[out_spec, out_spec],
        input_output_aliases={2: 0, 3: 1} if aliased else {},
        compiler_params=_params("parallel"),
        name="heads_layout",
    )(*args)


def _split_bf16(x):
    pieces = []
    rem = x
    for _ in range(N_DECAY_PIECES):
        p = rem.astype(BF16)
        pieces.append(p)
        rem = rem - p.astype(F32)
    return pieces


def _tri_ones(n, lower):
    row = lax.broadcasted_iota(jnp.int32, (n, n), 0)
    col = lax.broadcasted_iota(jnp.int32, (n, n), 1)
    keep = (col <= row) if lower else (row <= col)
    return jnp.where(keep, 1.0, 0.0).astype(BF16)


def _prompt_cumsum_kernel(lf_ref, c_ref, *, blk):
    t = lf_ref.shape[0]
    lower = _tri_ones(blk, lower=True)
    carry = jnp.zeros((1, lf_ref.shape[1]), F32)
    for i in range(t // blk):
        local = carry
        for piece in _split_bf16(lf_ref[i * blk:(i + 1) * blk, :]):
            local = local + jnp.dot(lower, piece, preferred_element_type=F32)
        c_ref[i * blk:(i + 1) * blk, :] = local
        carry = local[blk - 1:blk, :]


def _prompt_cumsum(logf, *, batch, seq):
    w = logf.shape[1]
    return pl.pallas_call(
        functools.partial(_prompt_cumsum_kernel, blk=V7X_LANES),
        out_shape=jax.ShapeDtypeStruct((batch * seq, w), F32),
        grid=(batch,),
        in_specs=[pl.BlockSpec((seq, w), lambda b: (b, 0))],
        out_specs=pl.BlockSpec((seq, w), lambda b: (b, 0)),
        compiler_params=_params("parallel"),
        name="prompt_cumsum",
    )(logf)


def _lane_cumsum_block(xt, upper):
    out = None
    for piece in _split_bf16(xt):
        part = jnp.dot(piece, upper, preferred_element_type=F32)
        out = part if out is None else out + part
    return out


def _regroup(c, spread, valid):
    pieces = jnp.concatenate([p.astype(F32) for p in _split_bf16(c)], axis=0).astype(BF16)
    y = jnp.dot(pieces, spread, preferred_element_type=F32)
    rows = [jnp.sum(jnp.where(v, y, 0.0), axis=0, keepdims=True) for v in valid]
    return jnp.concatenate(rows, axis=0)


def _decode_cumsum_kernel(pt_ref, *refs, n_pg, heads, page):
    page_refs = refs[:n_pg]
    new_ref = refs[n_pg]
    c_ref, cn_ref, carry_ref = refs[n_pg + 1:]
    g = pl.program_id(1)
    per = heads // HEAD_GROUPS
    nt = new_ref.shape[0]

    @pl.when(g == 0)
    def _():
        carry_ref[...] = jnp.zeros_like(carry_ref)

    def masks(m, head_slot):
        head = lax.rem(lax.broadcasted_iota(jnp.int32, (N_DECAY_PIECES * heads, m), 0), heads)
        col = lax.broadcasted_iota(jnp.int32, (N_DECAY_PIECES * heads, m), 1)
        owns = (head // HEAD_GROUPS) == head_slot(col)
        return [jnp.logical_and(lax.rem(head, HEAD_GROUPS) == grp, owns) for grp in range(HEAD_GROUPS)]

    m_past = page * per
    t_idx = lax.broadcasted_iota(jnp.int32, (page, m_past), 0)
    c_idx = lax.broadcasted_iota(jnp.int32, (page, m_past), 1)
    spread = jnp.where(c_idx // per == t_idx, 1.0, 0.0).astype(BF16)
    valid = masks(m_past, lambda col: lax.rem(col, per))

    upper = _tri_ones(page, lower=False)
    carry = carry_ref[...]
    for i in range(n_pg):
        local = _lane_cumsum_block(page_refs[i][...], upper) + carry
        c_ref[0, :, i * m_past:(i + 1) * m_past] = _regroup(local, spread, valid)
        carry = local[:, page - 1:page]
    carry_ref[...] = carry

    @pl.when(g == pl.num_programs(1) - 1)
    def _():
        xn = new_ref[...]
        xn = jnp.concatenate([xn, jnp.zeros((V7X_LANES - nt, xn.shape[1]), F32)], axis=0)
        local = _lane_cumsum_block(xn.T[:heads, :], upper) + carry
        t_new = lax.broadcasted_iota(jnp.int32, (V7X_LANES, V7X_LANES), 0)
        c_new = lax.broadcasted_iota(jnp.int32, (V7X_LANES, V7X_LANES), 1)
        hit = jnp.logical_and(lax.rem(c_new, nt) == t_new, c_new < per * nt)
        cn_ref[0] = _regroup(local, jnp.where(hit, 1.0, 0.0).astype(BF16),
                             masks(V7X_LANES, lambda col: col // nt))


def _decode_cumsum(page_table, cache_logf, layer, logf, *, n_prompt_rows, heads):
    bd, n_pages = page_table.shape
    page = cache_logf.shape[3]
    per = heads // HEAD_GROUPS
    assert heads % HEAD_GROUPS == 0 and per * SAMPLE_ROWS <= V7X_LANES and page == V7X_LANES
    n_pg = _tile(n_pages, 16, 1)
    past = n_pages * page

    def page_spec(i):
        return pl.BlockSpec((None, None, heads, page),
                            lambda b, g, pt: (layer, pt[b, g * n_pg + i], 0, 0))

    new_block = n_prompt_rows // SAMPLE_ROWS
    grid_spec = pltpu.PrefetchScalarGridSpec(
        num_scalar_prefetch=1,
        grid=(bd, n_pages // n_pg),
        in_specs=[page_spec(i) for i in range(n_pg)]
        + [pl.BlockSpec((SAMPLE_ROWS, logf.shape[1]), lambda b, g, pt: (new_block + b, 0))],
        out_specs=[
            pl.BlockSpec((1, HEAD_GROUPS, n_pg * page * per), lambda b, g, pt: (b, 0, g)),
            pl.BlockSpec((1, HEAD_GROUPS, V7X_LANES), lambda b, g, pt: (b, 0, 0)),
        ],
        scratch_shapes=[pltpu.VMEM((heads, 1), F32)],
    )
    return pl.pallas_call(
        functools.partial(_decode_cumsum_kernel, n_pg=n_pg, heads=heads, page=page),
        out_shape=[jax.ShapeDtypeStruct((bd, HEAD_GROUPS, past * per), F32),
                   jax.ShapeDtypeStruct((bd, HEAD_GROUPS, V7X_LANES), F32)],
        grid_spec=grid_spec,
        compiler_params=_params("parallel", "arbitrary"),
        name="decode_cumsum",
    )(page_table, *([cache_logf] * n_pg), logf)


def _fox_prompt_kernel(q_ref, k_ref, v_ref, c_ref, o_ref, kaug_ref, vt_ref, acc_ref, *, tq, dh, scale):
    qi = pl.program_id(2)
    hps, nk = vt_ref.shape[0], vt_ref.shape[1]
    lanes = c_ref.shape[1]

    @pl.when(qi == 0)
    def _():
        row = lax.broadcasted_iota(jnp.int32, (lanes, dh), 0)
        col = lax.broadcasted_iota(jnp.int32, (lanes, dh), 1)
        for j in range(nk):
            sl = slice(j * tq, (j + 1) * tq)
            pieces = _split_bf16(c_ref[sl, :] * LOG2E)
            for u in range(hps):
                head = pl.program_id(1) * hps + u
                hc = slice(u * dh, (u + 1) * dh)
                aug = jnp.zeros((tq, dh), F32)
                for i, piece in enumerate(pieces):
                    place = jnp.where(jnp.logical_and(row == head, col == i), 1.0, 0.0).astype(BF16)
                    aug = aug + jnp.dot(piece, place, preferred_element_type=F32)
                kaug_ref[u, j, :, :dh] = k_ref[sl, hc].astype(BF16)
                kaug_ref[u, j, :, dh:] = aug.astype(BF16)
                vt_ref[u, j] = v_ref[sl, hc].T.astype(BF16)

    lane = lax.broadcasted_iota(jnp.int32, (tq, dh), 1)
    minus_ones = jnp.where(lane < N_DECAY_PIECES, -1.0, 0.0).astype(BF16)
    q_aug = [jnp.concatenate([(q_ref[:, u * dh:(u + 1) * dh] * (scale * LOG2E)).astype(BF16), minus_ones],
                             axis=1) for u in range(hps)]

    def block(kj, carry, diagonal):
        out = []
        for u in range(hps):
            m, l = carry[2 * u], carry[2 * u + 1]
            s = _nt_dot(kaug_ref[u, kj], q_aug[u])
            if diagonal:
                key = lax.broadcasted_iota(jnp.int32, (tq, tq), 0)
                qry = lax.broadcasted_iota(jnp.int32, (tq, tq), 1)
                s = jnp.where(key <= qry, s, NEG_BIG)
            m_new = jnp.maximum(m, jnp.max(s, axis=0, keepdims=True))
            alpha = jnp.exp2(m - m_new)
            p = jnp.exp2(s - m_new)
            l = alpha * l + jnp.sum(p, axis=0, keepdims=True)
            acc_ref[u] = alpha * acc_ref[u] + jnp.dot(vt_ref[u, kj], p.astype(BF16),
                                                      preferred_element_type=F32)
            out += [m_new, l]
        return tuple(out)

    acc_ref[...] = jnp.zeros_like(acc_ref)
    init = (jnp.full((1, tq), NEG_BIG, F32), jnp.zeros((1, tq), F32)) * hps
    carry = lax.fori_loop(0, qi, lambda kj, c: block(kj, c, False), init)
    carry = block(qi, carry, True)
    for u in range(hps):
        o_ref[:, u * dh:(u + 1) * dh] = (acc_ref[u] / carry[2 * u + 1]).T.astype(o_ref.dtype)


def _fox_prompt(proj, c_col, *, rows, batch, seq, heads, dh, tq, hps):
    nq = seq // tq
    hb = heads // hps
    w = hps * dh
    return pl.pallas_call(
        functools.partial(_fox_prompt_kernel, tq=tq, dh=dh, scale=dh ** -0.5),
        out_shape=jax.ShapeDtypeStruct((rows, heads * dh), BF16),
        grid=(batch, hb, nq),
        in_specs=[
            pl.BlockSpec((tq, w), lambda b, h, i: (b * nq + i, h)),
            pl.BlockSpec((seq, w), lambda b, h, i: (b, hb + h)),
            pl.BlockSpec((seq, w), lambda b, h, i: (b, 2 * hb + h)),
            pl.BlockSpec((seq, c_col.shape[1]), lambda b, h, i: (b, 0)),
        ],
        out_specs=pl.BlockSpec((tq, w), lambda b, h, i: (b * nq + i, h)),
        scratch_shapes=[pltpu.VMEM((hps, nq, tq, 2 * dh), BF16),
                        pltpu.VMEM((hps, nq, dh, tq), BF16),
                        pltpu.VMEM((hps, dh, tq), F32)],
        compiler_params=_params("parallel", "parallel", "arbitrary"),
        name="fox_prompt",
    )(proj, proj, proj, c_col)


def _fox_decode_kernel(pt_ref, q_ref, kn_ref, vn_ref, ck_ref, cn_ref, *refs,
                       n_pg, heads, dh, page, n_new, scale):
    k_refs = refs[:n_pg]
    v_refs = refs[n_pg:2 * n_pg]
    o_ref, m_ref, l_ref, acc_ref = refs[2 * n_pg + 1:]
    step = pl.program_id(1)
    nt = q_ref.shape[0]
    per = heads // HEAD_GROUPS
    gr = per * nt
    cols = page * per

    @pl.when(step == 0)
    def _():
        m_ref[...] = jnp.full_like(m_ref, NEG_BIG)
        l_ref[...] = jnp.zeros_like(l_ref)
        acc_ref[...] = jnp.zeros_like(acc_ref)

    def group_rows(ref, grp):
        return ref[pl.ds(grp, cols, stride=HEAD_GROUPS), :].astype(BF16)

    def group_queries(grp):
        parts = [q_ref[:, (grp + HEAD_GROUPS * j) * dh:(grp + HEAD_GROUPS * j + 1) * dh] for j in range(per)]
        return (jnp.concatenate(parts, axis=0) * scale).astype(BF16)

    def update(grp, s, v_blocks, width):
        m_old = m_ref[grp]
        m_new = jnp.maximum(m_old, jnp.max(s, axis=-1, keepdims=True))
        alpha = jnp.exp(m_old - m_new)
        p = jnp.exp(s - m_new)
        l_ref[grp] = alpha * l_ref[grp] + jnp.sum(p, axis=-1, keepdims=True)
        pv = None
        for i, vb in enumerate(v_blocks):
            w = p[:, i * width:(i + 1) * width].astype(BF16)
            part = jnp.dot(w, vb, preferred_element_type=F32)
            pv = part if pv is None else pv + part
        acc_ref[grp] = alpha * acc_ref[grp] + pv
        m_ref[grp] = m_new

    row = lax.broadcasted_iota(jnp.int32, (gr, n_pg * cols), 0)
    col = lax.broadcasted_iota(jnp.int32, (gr, n_pg * cols), 1)
    own = (row // nt) == lax.rem(col, per)
    for grp in range(HEAD_GROUPS):
        qg = group_queries(grp)
        s = jnp.concatenate([_nt_dot(qg, group_rows(k_refs[i], grp)) for i in range(n_pg)], axis=1)
        s = jnp.where(own, s - ck_ref[0, grp:grp + 1, :], NEG_BIG)
        update(grp, s, [group_rows(v_refs[i], grp) for i in range(n_pg)], cols)

    @pl.when(step == pl.num_programs(1) - 1)
    def _():
        row = lax.broadcasted_iota(jnp.int32, (gr, V7X_LANES), 0)
        col = lax.broadcasted_iota(jnp.int32, (gr, V7X_LANES), 1)
        t_col = lax.rem(col, nt)
        visible = jnp.logical_and(jnp.logical_and((row // nt) == (col // nt), t_col <= lax.rem(row, nt)),
                                  jnp.logical_and(t_col < n_new, col < gr))
        pad = jnp.zeros((V7X_LANES - gr, dh), F32)
        for grp in range(HEAD_GROUPS):
            head_cols = [slice((grp + HEAD_GROUPS * j) * dh, (grp + HEAD_GROUPS * j + 1) * dh) for j in range(per)]
            k_new = jnp.concatenate([kn_ref[:, c] for c in head_cols] + [pad], axis=0).astype(BF16)
            v_new = jnp.concatenate([vn_ref[:, c] for c in head_cols] + [pad], axis=0).astype(BF16)
            s = _nt_dot(group_queries(grp), k_new)
            s = jnp.where(visible, s - cn_ref[0, grp:grp + 1, :], NEG_BIG)
            update(grp, s, [v_new], V7X_LANES)
            out = acc_ref[grp] / l_ref[grp]
            for j, c in enumerate(head_cols):
                o_ref[:, c] = out[j * nt:(j + 1) * nt].astype(o_ref.dtype)


def _fox_decode(page_table, proj, ck_past, c_new, cache_k, cache_v, layer, attn, *,
                n_prompt_rows, heads, dh, n_new, n_pg):
    bd, n_pages = page_table.shape
    page = cache_k.shape[2] // heads
    per = heads // HEAD_GROUPS
    gr = per * SAMPLE_ROWS
    hd = heads * dh
    sblk = n_prompt_rows // SAMPLE_ROWS

    def row_spec(col):
        return pl.BlockSpec((SAMPLE_ROWS, hd), lambda b, g, pt: (sblk + b, col))

    def page_spec(i):
        return pl.BlockSpec((None, None, page * heads, dh),
                            lambda b, g, pt: (layer, pt[b, g * n_pg + i], 0, 0))

    n_in = 5 + 2 * n_pg
    grid_spec = pltpu.PrefetchScalarGridSpec(
        num_scalar_prefetch=1,
        grid=(bd, n_pages // n_pg),
        in_specs=[row_spec(0), row_spec(1), row_spec(2),
                  pl.BlockSpec((1, HEAD_GROUPS, n_pg * page * per), lambda b, g, pt: (b, 0, g)),
                  pl.BlockSpec((1, HEAD_GROUPS, V7X_LANES), lambda b, g, pt: (b, 0, 0))]
        + [page_spec(i) for i in range(n_pg)] * 2
        + [pl.BlockSpec(memory_space=pl.ANY)],
        out_specs=pl.BlockSpec((SAMPLE_ROWS, hd), lambda b, g, pt: (sblk + b, 0)),
        scratch_shapes=[pltpu.VMEM((HEAD_GROUPS, gr, 1), F32),
                        pltpu.VMEM((HEAD_GROUPS, gr, 1), F32),
                        pltpu.VMEM((HEAD_GROUPS, gr, dh), F32)],
    )
    return pl.pallas_call(
        functools.partial(_fox_decode_kernel, n_pg=n_pg, heads=heads, dh=dh, page=page,
                          n_new=n_new, scale=dh ** -0.5),
        out_shape=jax.ShapeDtypeStruct(attn.shape, attn.dtype),
        grid_spec=grid_spec,
        input_output_aliases={n_in + 1: 0},
        compiler_params=_params("parallel", "arbitrary"),
        name="fox_decode",
    )(page_table, proj, proj, proj, ck_past, c_new,
      *([cache_k] * n_pg), *([cache_v] * n_pg), attn)


def _ret_tables(heads, chunk, t_real):
    lg = jnp.log1p(-jnp.exp2(-5.0 - jnp.arange(heads, dtype=F32)))
    n = jnp.arange(chunk, dtype=F32)
    diff = n[:, None] - n[None, :]
    causal = diff >= 0
    dmat = jnp.where(causal[None], jnp.exp(jnp.where(causal, diff, 0.0)[None] * lg[:, None, None]), 0.0)
    cross = jnp.exp((n[None, :] + 1.0) * lg[:, None])
    kdec = jnp.where(n[None, :] < t_real, jnp.exp((t_real - 1.0 - n)[None, :] * lg[:, None]), 0.0)
    full = jnp.broadcast_to(jnp.exp(t_real * lg)[:, None], (heads, chunk))
    dvec = jnp.stack([cross, kdec, full], axis=-1)
    return dmat.astype(F32), dvec.astype(F32)


def _rope_tables(positions, dk):
    half = dk // 2
    inv = ROPE_BASE ** (-jnp.arange(half, dtype=F32) / half)
    ang = positions.astype(F32)[:, None] * inv[None, :]
    return jnp.cos(ang), jnp.sin(ang)


def _ret_kernel(*refs, has_state, chunk, rows, n_chunks, dk):
    if has_state:
        q_ref, k_ref, v_ref, g_ref, cos_ref, sin_ref, dmat_ref, dvec_ref, s0_ref, y_ref, st_ref = refs
        st_ref[...] = s0_ref[...]
    else:
        q_ref, k_ref, v_ref, g_ref, cos_ref, sin_ref, dmat_ref, dvec_ref, y_ref, st_ref = refs
        st_ref[...] = jnp.zeros_like(st_ref)
    half = dk // 2

    def padded(x):
        x = x.astype(F32)
        if rows == chunk:
            return x
        return jnp.concatenate([x, jnp.zeros((chunk - rows, x.shape[1]), F32)], axis=0)

    dvec = dvec_ref[0]
    cross_d = dvec[:, 0:1]
    k_d = dvec[:, 1:2]
    full_d = dvec[0:1, 2:3]
    dmat = dmat_ref[0]

    def step(c, carry):
        sl = pl.ds(pl.multiple_of(c * rows, rows), rows)
        cos = padded(cos_ref[sl, :])
        sin = padded(sin_ref[sl, :])

        def rope(x):
            x1 = x[:, :half]
            x2 = x[:, half:]
            return jnp.concatenate([x1 * cos - x2 * sin, x1 * sin + x2 * cos], axis=1)

        q = rope(padded(q_ref[sl, :]))
        k = rope(padded(k_ref[sl, :])) * (dk ** -0.5)
        v = padded(v_ref[sl, :]).astype(BF16)
        qb = q.astype(BF16)
        s = _nt_dot(qb, k.astype(BF16)) * dmat
        state = st_ref[0, 0]
        y = jnp.dot(s.astype(BF16), v, preferred_element_type=F32)
        y = y + jnp.dot(qb, state.astype(BF16), preferred_element_type=F32) * cross_d
        kdec_t = (k * k_d).T.astype(BF16)
        st_ref[0, 0] = full_d * state + jnp.dot(kdec_t, v, preferred_element_type=F32)

        mu = jnp.mean(y, axis=-1, keepdims=True)
        yc = y - mu
        var = jnp.mean(yc * yc, axis=-1, keepdims=True)
        yn = yc * lax.rsqrt(var + GN_EPS)
        gate = padded(g_ref[sl, :])
        out = yn * (gate * jax.nn.sigmoid(gate))
        y_ref[sl, :] = out[:rows].astype(y_ref.dtype)
        return carry

    lax.fori_loop(0, n_chunks, step, 0)


def _retention(proj, cos, sin, state0, layer, y_prev, *, row0, batch, n_chunks, rows, t_real,
               heads, dk, dv):
    chunk = max(rows, RET_CHUNK)
    dmat, dvec = _ret_tables(heads, chunk, t_real)
    seq_rows = n_chunks * rows
    r0 = row0 // seq_rows
    qk_w = heads * dk
    has_state = state0 is not None
    aliased = y_prev is not None

    def tok_spec(width, col0):
        cb = col0 // width
        return pl.BlockSpec((seq_rows, width), lambda b, h: (r0 + b, cb + h))

    in_specs = [
        tok_spec(dk, 0), tok_spec(dk, qk_w), tok_spec(dv, 2 * qk_w), tok_spec(dv, 2 * qk_w + heads * dv),
        pl.BlockSpec((seq_rows, dk // 2), lambda b, h: (0, 0)),
        pl.BlockSpec((seq_rows, dk // 2), lambda b, h: (0, 0)),
        pl.BlockSpec((1, chunk, chunk), lambda b, h: (h, 0, 0)),
        pl.BlockSpec((1, chunk, 3), lambda b, h: (h, 0, 0)),
    ]
    args = [proj, proj, proj, proj, cos, sin, dmat, dvec]
    if has_state:
        in_specs.append(pl.BlockSpec((None, 1, 1, dk, dv), lambda b, h: (layer, b, h, 0, 0)))
        args.append(state0)
    n_used = len(args)
    if aliased:
        in_specs.append(pl.BlockSpec(memory_space=pl.ANY))
        args.append(y_prev)
    n_in = len(args)

    def kernel(*refs):
        refs = refs[:n_used] + refs[n_in:]
        _ret_kernel(*refs, has_state=has_state, chunk=chunk, rows=rows, n_chunks=n_chunks, dk=dk)

    return pl.pallas_call(
        kernel,
        out_shape=[jax.ShapeDtypeStruct((proj.shape[0], heads * dv), BF16),
                   jax.ShapeDtypeStruct((batch, heads, dk, dv), F32)],
        grid=(batch, heads),
        in_specs=in_specs,
        out_specs=[
            pl.BlockSpec((seq_rows, dv), lambda b, h: (r0 + b, h)),
            pl.BlockSpec((1, 1, dk, dv), lambda b, h: (b, h, 0, 0)),
        ],
        input_output_aliases={n_in - 1: 0} if aliased else {},
        compiler_params=_params("parallel", "parallel"),
        name="retention",
    )(*args)


def kernel(x_prompt, x_sample, cache_fox_k, cache_fox_v, cache_fox_logf, state_ret, page_table, norm_ffn1, ffn1_w_gate, ffn1_w_up, ffn1_w_down, norm_mix, fox_w_in, fox_b_f, fox_w_out, ret_w_in, ret_w_out, norm_ffn2, ffn2_w_gate, ffn2_w_up, ffn2_w_down, norm_final):
    batch, seq, d = x_prompt.shape
    bd, n_new, _ = x_sample.shape
    depth = norm_ffn1.shape[0]
    f = ffn1_w_gate.shape[2]
    n_fox, pool, page, fox_heads, dh = cache_fox_k.shape
    ret_heads, dk, dv = state_ret.shape[2:]
    n_pages = page_table.shape[1]
    past = n_pages * page
    assert n_new <= SAMPLE_ROWS and fox_heads * dh == d and ret_heads * dk == d

    n_p = batch * seq
    n_s = bd * SAMPLE_ROWS
    rows = n_p + n_s
    tm = _tile(rows, 1100, V7X_BF16_SUBLANES)
    tf = _tile(f, V7X_MXU_DIM, V7X_LANES)
    tn = _tile(d, 512, V7X_LANES)
    tq = _tile(seq, 512, V7X_LANES)
    tr = _tile(n_p, 512, n_s)
    n_pg = _tile(n_pages, 4, 1)

    xs = jnp.pad(x_sample, ((0, 0), (0, SAMPLE_ROWS - n_new), (0, 0)))
    x = jnp.concatenate([x_prompt.reshape(n_p, d), xs.reshape(n_s, d)], axis=0)

    cache_k2 = cache_fox_k.reshape(n_fox, pool, page * fox_heads, dh)
    cache_v2 = cache_fox_v.reshape(n_fox, pool, page * fox_heads, dh)
    cache_logf_t = jnp.swapaxes(cache_fox_logf, 2, 3)
    fox_bias = jnp.pad(fox_b_f, ((0, 0), (0, V7X_LANES - fox_heads)))
    fox_w_in_t = jnp.swapaxes(fox_w_in, 1, 2)

    cos_p, sin_p = _rope_tables(jnp.arange(seq), dk)
    cos_s, sin_s = _rope_tables(past + jnp.arange(SAMPLE_ROWS), dk)

    def split_rows(a):
        w = a.shape[1]
        return (a[:n_p].reshape(batch, seq, w),
                a[n_p:].reshape(bd, SAMPLE_ROWS, w)[:, :n_new])

    kv_p = kv_s = None
    lp_l, ls_l, sp_l, ss_l = [], [], [], []
    for i in range(depth):
        wgu, wd = _ffn_weights(ffn1_w_gate, ffn1_w_up, ffn1_w_down, i, tf=tf)
        x = _ffn(x, norm_ffn1[i][None], wgu, wd, tm=tm, tf=tf)
        j = i // 2
        g_mix = norm_mix[i][None]
        if i % 2 == 0:
            proj = _norm_matmul(x, g_mix, fox_w_in_t, j, n=3 * d, tm=tm, tn=tn, out_dtype=F32,
                                w_transposed=True)
            logf = _forget_gate(x, g_mix, fox_w_in_t, j, fox_bias[j][None], row0=3 * d, heads=fox_heads, tm=tm)
            c_col = _prompt_cumsum(logf, batch=batch, seq=seq)
            attn = _fox_prompt(proj, c_col, rows=rows, batch=batch, seq=seq, heads=fox_heads, dh=dh, tq=tq,
                               hps=FOX_HEADS_PER_STEP)
            c_past, c_new = _decode_cumsum(page_table, cache_logf_t, j, logf,
                                           n_prompt_rows=n_p, heads=fox_heads)
            attn = _fox_decode(page_table, proj, c_past, c_new, cache_k2, cache_v2, j, attn,
                               n_prompt_rows=n_p, heads=fox_heads, dh=dh, n_new=n_new, n_pg=n_pg)
            x = _matmul_res(attn, fox_w_out, j, x, tm=tm, tn=tn)
            kv_p = _heads_layout(proj, kv_p, j, n_fox, row0=0, nrows=n_p, tr=tr, heads=fox_heads, dh=dh)
            kv_s = _heads_layout(proj, kv_s, j, n_fox, row0=n_p, nrows=n_s, tr=n_s, heads=fox_heads, dh=dh)
            lp, ls = split_rows(logf[:, :fox_heads])
            lp_l.append(lp)
            ls_l.append(ls)
        else:
            proj = _norm_matmul(x, g_mix, ret_w_in, j, n=ret_w_in.shape[2], tm=tm, tn=tn, out_dtype=BF16)
            y, sp = _retention(proj, cos_p, sin_p, None, j, None, row0=0, batch=batch,
                               n_chunks=seq // RET_CHUNK, rows=RET_CHUNK, t_real=RET_CHUNK,
                               heads=ret_heads, dk=dk, dv=dv)
            y, ss = _retention(proj, cos_s, sin_s, state_ret, j, y, row0=n_p, batch=bd,
                               n_chunks=1, rows=SAMPLE_ROWS, t_real=n_new,
                               heads=ret_heads, dk=dk, dv=dv)
            x = _matmul_res(y, ret_w_out, j, x, tm=tm, tn=tn)
            sp_l.append(sp)
            ss_l.append(ss)
        wgu, wd = _ffn_weights(ffn2_w_gate, ffn2_w_up, ffn2_w_down, i, tf=tf)
        x = _ffn(x, norm_ffn2[i][None], wgu, wd, tm=tm, tf=tf)

    g_fin = norm_final[None]
    y_prompt = _final_norm(x, g_fin, row0=0, nrows=n_p, tr=tr).reshape(batch, seq, d)
    y_sample = _final_norm(x, g_fin, row0=n_p, nrows=n_s, tr=n_s).reshape(bd, SAMPLE_ROWS, d)[:, :n_new]

    def heads_out(a, lead, n_tok):
        return a.reshape((n_fox,) + lead + (fox_heads, dh))[:, :, :n_tok]

    return (y_prompt, y_sample,
            heads_out(kv_p[0], (batch, seq), seq), heads_out(kv_p[1], (batch, seq), seq),
            jnp.stack(lp_l),
            heads_out(kv_s[0], (bd, SAMPLE_ROWS), n_new), heads_out(kv_s[1], (bd, SAMPLE_ROWS), n_new),
            jnp.stack(ls_l),
            jnp.stack(sp_l), jnp.stack(ss_l))
```

```python
import functools
import math

import jax
import jax.numpy as jnp
from jax import lax
from jax.experimental import pallas as pl
from jax.experimental.pallas import tpu as pltpu

F32 = jnp.float32
BF16 = jnp.bfloat16

NORM_EPS = 1e-6
GN_EPS = 1e-5
ROPE_BASE = 10000.0
NEG_BIG = -1e30
LOG2E = math.log2(math.e)

V7X_VMEM_BYTES = 64 * 1024 * 1024
V7X_LANES = 128
V7X_BF16_SUBLANES = 16
V7X_MXU_DIM = 256
VMEM_LIMIT = (V7X_VMEM_BYTES * 7) // 8

SAMPLE_ROWS = V7X_BF16_SUBLANES
RET_CHUNK = 128
RET_UNROLL = 4
N_DECAY_PIECES = 3
HEAD_GROUPS = 4
DECODE_PAGES_PER_STEP = 8
FOX_HEADS_PER_STEP = 2


def _tile(n, target, align):
    for t in range(min(n, target), 0, -1):
        if n % t == 0 and t % align == 0:
            return t
    raise ValueError(f"no tile for {n} (target {target}, align {align})")


def _params(*sem):
    return pltpu.CompilerParams(dimension_semantics=sem, vmem_limit_bytes=VMEM_LIMIT)


def _rms_norm(x, g):
    ms = jnp.mean(x * x, axis=-1, keepdims=True)
    return (x * lax.rsqrt(ms + NORM_EPS)) * g


def _nt_dot(a, b):
    return lax.dot_general(a, b, (((1,), (1,)), ((), ())), preferred_element_type=F32)


def _ffn_kernel(x_ref, g_ref, wg_ref, wu_ref, wd_ref, o_ref, h_ref, *, tf):
    @pl.when(pl.program_id(1) == 0)
    def _():
        x = x_ref[...]
        h_ref[...] = _rms_norm(x, g_ref[...]).astype(BF16)
        o_ref[...] = x

    wgu = jnp.concatenate([wg_ref[...].astype(BF16), wu_ref[...].astype(BF16)], axis=1)
    ab = jnp.dot(h_ref[...], wgu, preferred_element_type=F32)
    a = ab[:, :tf]
    b = ab[:, tf:]
    p = (a * jax.nn.sigmoid(a)) * (0.5 * b)
    o_ref[...] += jnp.dot(p.astype(BF16), wd_ref[...].astype(BF16), preferred_element_type=F32)


def _ffn(x, g, w_gate, w_up, w_down, layer, *, tm, tf):
    r, d = x.shape
    f = w_down.shape[1]
    return pl.pallas_call(
        functools.partial(_ffn_kernel, tf=tf),
        out_shape=jax.ShapeDtypeStruct((r, d), F32),
        grid=(r // tm, f // tf),
        in_specs=[
            pl.BlockSpec((tm, d), lambda i, j: (i, 0), pipeline_mode=pl.Buffered(1)),
            pl.BlockSpec((1, d), lambda i, j: (0, 0)),
            pl.BlockSpec((None, d, tf), lambda i, j: (layer, 0, j)),
            pl.BlockSpec((None, d, tf), lambda i, j: (layer, 0, j)),
            pl.BlockSpec((None, tf, d), lambda i, j: (layer, j, 0)),
        ],
        out_specs=pl.BlockSpec((tm, d), lambda i, j: (i, 0)),
        scratch_shapes=[pltpu.VMEM((tm, d), BF16)],
        compiler_params=_params("parallel", "arbitrary"),
        name="ffn",
    )(x, g, w_gate, w_up, w_down)


def _norm_matmul_kernel(x_ref, g_ref, w_ref, o_ref, h_ref, *, w_transposed):
    @pl.when(pl.program_id(1) == 0)
    def _():
        h_ref[...] = _rms_norm(x_ref[...], g_ref[...]).astype(BF16)

    w = w_ref[...].astype(BF16)
    if w_transposed:
        out = _nt_dot(h_ref[...], w)
    else:
        out = jnp.dot(h_ref[...], w, preferred_element_type=F32)
    o_ref[...] = out.astype(o_ref.dtype)


def _norm_matmul(x, g, w, layer, *, n, tm, tn, out_dtype, w_transposed=False):
    r, d = x.shape
    if w_transposed:
        w_spec = pl.BlockSpec((None, tn, d), lambda i, j: (layer, j, 0))
    else:
        w_spec = pl.BlockSpec((None, d, tn), lambda i, j: (layer, 0, j))
    return pl.pallas_call(
        functools.partial(_norm_matmul_kernel, w_transposed=w_transposed),
        out_shape=jax.ShapeDtypeStruct((r, n), out_dtype),
        grid=(r // tm, n // tn),
        in_specs=[
            pl.BlockSpec((tm, d), lambda i, j: (i, 0)),
            pl.BlockSpec((1, d), lambda i, j: (0, 0)),
            w_spec,
        ],
        out_specs=pl.BlockSpec((tm, tn), lambda i, j: (i, j)),
        scratch_shapes=[pltpu.VMEM((tm, d), BF16)],
        compiler_params=_params("parallel", "arbitrary"),
        name="norm_matmul",
    )(x, g, w)


def _forget_gate_kernel(x_ref, g_ref, w_ref, b_ref, o_ref, *, heads):
    h = _rms_norm(x_ref[...], g_ref[...]).astype(BF16)
    row = lax.broadcasted_iota(jnp.int32, w_ref.shape, 0)
    w = jnp.where(row < heads, w_ref[...], 0.0).astype(BF16)
    z = _nt_dot(h, w) + b_ref[...]
    o_ref[...] = -(jnp.maximum(-z, 0.0) + jnp.log1p(jnp.exp(-jnp.abs(z))))


def _forget_gate(x, g, w_t, layer, b_f, *, row0, heads, tm):
    r, d = x.shape
    n = V7X_LANES
    cb = row0 // n
    return pl.pallas_call(
        functools.partial(_forget_gate_kernel, heads=heads),
        out_shape=jax.ShapeDtypeStruct((r, n), F32),
        grid=(r // tm,),
        in_specs=[
            pl.BlockSpec((tm, d), lambda i: (i, 0)),
            pl.BlockSpec((1, d), lambda i: (0, 0)),
            pl.BlockSpec((None, n, d), lambda i: (layer, cb, 0)),
            pl.BlockSpec((1, n), lambda i: (0, 0)),
        ],
        out_specs=pl.BlockSpec((tm, n), lambda i: (i, 0)),
        compiler_params=_params("parallel"),
        name="forget_gate",
    )(x, g, w_t, b_f)


def _matmul_res_kernel(a_ref, w_ref, r_ref, o_ref):
    w = w_ref[...].astype(BF16)
    o_ref[...] = r_ref[...] + jnp.dot(a_ref[...], w, preferred_element_type=F32)


def _matmul_res(a, w, layer, res, *, tm, tn):
    r, k = a.shape
    n = w.shape[2]
    return pl.pallas_call(
        _matmul_res_kernel,
        out_shape=jax.ShapeDtypeStruct((r, n), F32),
        grid=(r // tm, n // tn),
        in_specs=[
            pl.BlockSpec((tm, k), lambda i, j: (i, 0)),
            pl.BlockSpec((None, k, tn), lambda i, j: (layer, 0, j)),
            pl.BlockSpec((tm, tn), lambda i, j: (i, j)),
        ],
        out_specs=pl.BlockSpec((tm, tn), lambda i, j: (i, j)),
        compiler_params=_params("parallel", "arbitrary"),
        name="matmul_res",
    )(a, w, res)


def _final_norm_kernel(x_ref, g_ref, o_ref):
    o_ref[...] = _rms_norm(x_ref[...], g_ref[...])


def _final_norm(x, g, *, row0, nrows, tr):
    d = x.shape[1]
    b0 = row0 // tr
    return pl.pallas_call(
        _final_norm_kernel,
        out_shape=jax.ShapeDtypeStruct((nrows, d), F32),
        grid=(nrows // tr,),
        in_specs=[pl.BlockSpec((tr, d), lambda i: (b0 + i, 0)), pl.BlockSpec((1, d), lambda i: (0, 0))],
        out_specs=pl.BlockSpec((tr, d), lambda i: (i, 0)),
        compiler_params=_params("parallel"),
        name="final_norm",
    )(x, g)


def _heads_layout_kernel(*refs, heads, dh, aliased):
    k_ref, v_ref = refs[:2]
    ko_ref, vo_ref = refs[2 + 2 * aliased:]
    tr = k_ref.shape[0]
    for src, dst in ((k_ref, ko_ref), (v_ref, vo_ref)):
        for h in range(heads):
            dst[pl.ds(h, tr, stride=heads), :] = src[:, h * dh:(h + 1) * dh]


def _heads_layout(proj, prev, layer, n_layers, *, row0, nrows, tr, heads, dh):
    d = heads * dh
    b0 = row0 // tr
    aliased = prev is not None
    shape = jax.ShapeDtypeStruct((n_layers, nrows * heads, dh), F32)
    in_specs = [pl.BlockSpec((tr, d), lambda i: (b0 + i, 1)),
                pl.BlockSpec((tr, d), lambda i: (b0 + i, 2))]
    args = [proj, proj]
    if aliased:
        in_specs += [pl.BlockSpec(memory_space=pl.ANY)] * 2
        args += list(prev)
    out_spec = pl.BlockSpec((None, tr * heads, dh), lambda i: (layer, i, 0))
    return pl.pallas_call(
        functools.partial(_heads_layout_kernel, heads=heads, dh=dh, aliased=aliased),
        out_shape=[shape, shape],
        grid=(nrows // tr,),
        in_specs=in_specs,
        out_specs=[out_spec, out_spec],
        input_output_aliases={2: 0, 3: 1} if aliased else {},
        compiler_params=_params("parallel"),
        name="heads_layout",
    )(*args)


def _split_bf16(x):
    pieces = []
    rem = x
    for _ in range(N_DECAY_PIECES):
        p = rem.astype(BF16)
        pieces.append(p)
        rem = rem - p.astype(F32)
    return pieces


def _tri_ones(n, lower):
    row = lax.broadcasted_iota(jnp.int32, (n, n), 0)
    col = lax.broadcasted_iota(jnp.int32, (n, n), 1)
    keep = (col <= row) if lower else (row <= col)
    return jnp.where(keep, 1.0, 0.0).astype(BF16)


def _prompt_cumsum_kernel(lf_ref, c_ref, *, blk):
    t = lf_ref.shape[0]
    lower = _tri_ones(blk, lower=True)
    carry = jnp.zeros((1, lf_ref.shape[1]), F32)
    for i in range(t // blk):
        local = carry
        for piece in _split_bf16(lf_ref[i * blk:(i + 1) * blk, :]):
            local = local + jnp.dot(lower, piece, preferred_element_type=F32)
        c_ref[i * blk:(i + 1) * blk, :] = local
        carry = local[blk - 1:blk, :]


def _prompt_cumsum(logf, *, batch, seq):
    w = logf.shape[1]
    return pl.pallas_call(
        functools.partial(_prompt_cumsum_kernel, blk=V7X_LANES),
        out_shape=jax.ShapeDtypeStruct((batch * seq, w), F32),
        grid=(batch,),
        in_specs=[pl.BlockSpec((seq, w), lambda b: (b, 0))],
        out_specs=pl.BlockSpec((seq, w), lambda b: (b, 0)),
        compiler_params=_params("parallel"),
        name="prompt_cumsum",
    )(logf)


def _lane_cumsum_block(xt, upper):
    out = None
    for piece in _split_bf16(xt):
        part = jnp.dot(piece, upper, preferred_element_type=F32)
        out = part if out is None else out + part
    return out


def _regroup(c, spread, valid, heads):
    y = None
    for piece in _split_bf16(c):
        part = jnp.dot(piece, spread, preferred_element_type=F32)
        y = part if y is None else y + part
    out = []
    for i in range(c.shape[0] // heads):
        blk = y[i * heads:(i + 1) * heads]
        rows = [jnp.sum(jnp.where(v, blk, 0.0), axis=0, keepdims=True) for v in valid]
        out.append(jnp.concatenate(rows, axis=0))
    return out


def _decode_cumsum_kernel(pt_ref, *refs, n_pg, heads, page):
    page_refs = refs[:n_pg]
    new_ref = refs[n_pg]
    c_ref, cn_ref, carry_ref = refs[n_pg + 1:]
    g = pl.program_id(1)
    per = heads // HEAD_GROUPS
    nt = new_ref.shape[0]
    nr = n_pg * heads

    @pl.when(g == 0)
    def _():
        carry_ref[...] = jnp.zeros_like(carry_ref)

    def masks(m, head_slot):
        head = lax.broadcasted_iota(jnp.int32, (heads, m), 0)
        col = lax.broadcasted_iota(jnp.int32, (heads, m), 1)
        owns = (head // HEAD_GROUPS) == head_slot(col)
        return [jnp.logical_and(lax.rem(head, HEAD_GROUPS) == grp, owns) for grp in range(HEAD_GROUPS)]

    m_past = page * per
    t_idx = lax.broadcasted_iota(jnp.int32, (page, m_past), 0)
    c_idx = lax.broadcasted_iota(jnp.int32, (page, m_past), 1)
    spread = jnp.where(c_idx // per == t_idx, 1.0, 0.0).astype(BF16)
    upper = _tri_ones(page, lower=False)

    x = jnp.concatenate([page_refs[i][...] for i in range(n_pg)], axis=0)
    within = _lane_cumsum_block(x, upper)
    totals = jnp.broadcast_to(within[:, page - 1:page], (nr, page))
    r_idx = lax.broadcasted_iota(jnp.int32, (nr, nr), 0)
    q_idx = lax.broadcasted_iota(jnp.int32, (nr, nr), 1)
    earlier_page = jnp.logical_and(lax.rem(r_idx, heads) == lax.rem(q_idx, heads), q_idx < r_idx)
    earlier_page = jnp.where(earlier_page, 1.0, 0.0).astype(BF16)
    offset = None
    for piece in _split_bf16(totals):
        part = jnp.dot(earlier_page, piece, preferred_element_type=F32)
        offset = part if offset is None else offset + part
    carry = carry_ref[...]
    local = within + offset + jnp.concatenate([carry] * n_pg, axis=0)
    carry = local[nr - heads:, page - 1:page]
    carry_ref[...] = carry
    grouped = _regroup(local, spread, masks(m_past, lambda col: lax.rem(col, per)), heads)
    for i in range(n_pg):
        c_ref[0, :, i * m_past:(i + 1) * m_past] = grouped[i]

    @pl.when(g == pl.num_programs(1) - 1)
    def _():
        xn = new_ref[...]
        xn = jnp.concatenate([xn, jnp.zeros((V7X_LANES - nt, xn.shape[1]), F32)], axis=0)
        new = _lane_cumsum_block(xn.T[:heads, :], upper) + carry
        t_new = lax.broadcasted_iota(jnp.int32, (V7X_LANES, V7X_LANES), 0)
        c_new = lax.broadcasted_iota(jnp.int32, (V7X_LANES, V7X_LANES), 1)
        hit = jnp.logical_and(lax.rem(c_new, nt) == t_new, c_new < per * nt)
        cn_ref[0] = _regroup(new, jnp.where(hit, 1.0, 0.0).astype(BF16),
                             masks(V7X_LANES, lambda col: col // nt), heads)[0]


def _decode_cumsum(page_table, cache_logf, layer, logf, *, n_prompt_rows, heads):
    bd, n_pages = page_table.shape
    page = cache_logf.shape[3]
    per = heads // HEAD_GROUPS
    assert heads % HEAD_GROUPS == 0 and per * SAMPLE_ROWS <= V7X_LANES and page == V7X_LANES
    n_pg = _tile(n_pages, 16, 1)
    past = n_pages * page

    def page_spec(i):
        return pl.BlockSpec((None, None, heads, page),
                            lambda b, g, pt: (layer, pt[b, g * n_pg + i], 0, 0))

    new_block = n_prompt_rows // SAMPLE_ROWS
    grid_spec = pltpu.PrefetchScalarGridSpec(
        num_scalar_prefetch=1,
        grid=(bd, n_pages // n_pg),
        in_specs=[page_spec(i) for i in range(n_pg)]
        + [pl.BlockSpec((SAMPLE_ROWS, logf.shape[1]), lambda b, g, pt: (new_block + b, 0))],
        out_specs=[
            pl.BlockSpec((1, HEAD_GROUPS, n_pg * page * per), lambda b, g, pt: (b, 0, g)),
            pl.BlockSpec((1, HEAD_GROUPS, V7X_LANES), lambda b, g, pt: (b, 0, 0)),
        ],
        scratch_shapes=[pltpu.VMEM((heads, 1), F32)],
    )
    return pl.pallas_call(
        functools.partial(_decode_cumsum_kernel, n_pg=n_pg, heads=heads, page=page),
        out_shape=[jax.ShapeDtypeStruct((bd, HEAD_GROUPS, past * per), F32),
                   jax.ShapeDtypeStruct((bd, HEAD_GROUPS, V7X_LANES), F32)],
        grid_spec=grid_spec,
        compiler_params=_params("parallel", "arbitrary"),
        name="decode_cumsum",
    )(page_table, *([cache_logf] * n_pg), logf)


def _fox_prompt_kernel(q_ref, k_ref, v_ref, c_ref, o_ref, kaug_ref, vt_ref, acc_ref, *, tq, dh, scale):
    qi = pl.program_id(2)
    hps, nk = vt_ref.shape[0], vt_ref.shape[1]
    lanes = c_ref.shape[1]

    @pl.when(qi == 0)
    def _():
        row = lax.broadcasted_iota(jnp.int32, (lanes, dh), 0)
        col = lax.broadcasted_iota(jnp.int32, (lanes, dh), 1)
        for j in range(nk):
            sl = slice(j * tq, (j + 1) * tq)
            pieces = _split_bf16(c_ref[sl, :] * LOG2E)
            for u in range(hps):
                head = pl.program_id(1) * hps + u
                hc = slice(u * dh, (u + 1) * dh)
                aug = jnp.zeros((tq, dh), F32)
                for i, piece in enumerate(pieces):
                    place = jnp.where(jnp.logical_and(row == head, col == i), 1.0, 0.0).astype(BF16)
                    aug = aug + jnp.dot(piece, place, preferred_element_type=F32)
                kaug_ref[u, j, :, :dh] = k_ref[sl, hc].astype(BF16)
                kaug_ref[u, j, :, dh:] = aug.astype(BF16)
                vt_ref[u, j] = v_ref[sl, hc].T.astype(BF16)

    lane = lax.broadcasted_iota(jnp.int32, (tq, dh), 1)
    minus_ones = jnp.where(lane < N_DECAY_PIECES, -1.0, 0.0).astype(BF16)
    q_aug = [jnp.concatenate([(q_ref[:, u * dh:(u + 1) * dh] * (scale * LOG2E)).astype(BF16), minus_ones],
                             axis=1) for u in range(hps)]

    def absorb(kj, m, l, u, diagonal):
        s = _nt_dot(kaug_ref[u, kj], q_aug[u])
        if diagonal:
            key = lax.broadcasted_iota(jnp.int32, (tq, tq), 0)
            qry = lax.broadcasted_iota(jnp.int32, (tq, tq), 1)
            s = jnp.where(key <= qry, s, NEG_BIG)
        m_new = jnp.maximum(m, jnp.max(s, axis=0, keepdims=True))
        alpha = jnp.exp2(m - m_new)
        p = jnp.exp2(s - m_new)
        l = alpha * l + jnp.sum(p, axis=0, keepdims=True)
        pv = jnp.dot(vt_ref[u, kj], p.astype(BF16), preferred_element_type=F32)
        acc_ref[u] = pv if kj == 0 else alpha * acc_ref[u] + pv
        return m_new, l

    for last in range(nk):
        @pl.when(qi == last)
        def _(last=last):
            for u in range(hps):
                m = jnp.full((1, tq), NEG_BIG, F32)
                l = jnp.zeros((1, tq), F32)
                for kj in range(last + 1):
                    m, l = absorb(kj, m, l, u, kj == last)
                o_ref[:, u * dh:(u + 1) * dh] = (acc_ref[u] / l).T.astype(o_ref.dtype)


def _fox_prompt(proj, c_col, *, rows, batch, seq, heads, dh, tq, hps):
    nq = seq // tq
    hb = heads // hps
    w = hps * dh
    return pl.pallas_call(
        functools.partial(_fox_prompt_kernel, tq=tq, dh=dh, scale=dh ** -0.5),
        out_shape=jax.ShapeDtypeStruct((rows, heads * dh), BF16),
        grid=(batch, hb, nq),
        in_specs=[
            pl.BlockSpec((tq, w), lambda b, h, i: (b * nq + i, h)),
            pl.BlockSpec((seq, w), lambda b, h, i: (b, hb + h)),
            pl.BlockSpec((seq, w), lambda b, h, i: (b, 2 * hb + h)),
            pl.BlockSpec((seq, c_col.shape[1]), lambda b, h, i: (b, 0)),
        ],
        out_specs=pl.BlockSpec((tq, w), lambda b, h, i: (b * nq + i, h)),
        scratch_shapes=[pltpu.VMEM((hps, nq, tq, 2 * dh), BF16),
                        pltpu.VMEM((hps, nq, dh, tq), BF16),
                        pltpu.VMEM((hps, dh, tq), F32)],
        compiler_params=_params("parallel", "parallel", "arbitrary"),
        name="fox_prompt",
    )(proj, proj, proj, c_col)


def _fox_decode_kernel(pt_ref, q_ref, kn_ref, vn_ref, ck_ref, cn_ref, *refs,
                       n_pg, heads, dh, page, n_new, scale):
    k_refs = refs[:n_pg]
    v_refs = refs[n_pg:2 * n_pg]
    o_ref, m_ref, l_ref, acc_ref = refs[2 * n_pg + 1:]
    step = pl.program_id(1)
    nt = q_ref.shape[0]
    per = heads // HEAD_GROUPS
    gr = per * nt
    cols = page * per

    @pl.when(step == 0)
    def _():
        m_ref[...] = jnp.full_like(m_ref, NEG_BIG)
        l_ref[...] = jnp.zeros_like(l_ref)
        acc_ref[...] = jnp.zeros_like(acc_ref)

    def group_rows(ref, grp):
        return ref[pl.ds(grp, cols, stride=HEAD_GROUPS), :].astype(BF16)

    def group_queries(grp):
        parts = [q_ref[:, (grp + HEAD_GROUPS * j) * dh:(grp + HEAD_GROUPS * j + 1) * dh] for j in range(per)]
        return (jnp.concatenate(parts, axis=0) * scale).astype(BF16)

    def update(grp, s, v_blocks, width):
        m_old = m_ref[grp]
        m_new = jnp.maximum(m_old, jnp.max(s, axis=-1, keepdims=True))
        alpha = jnp.exp(m_old - m_new)
        p = jnp.exp(s - m_new)
        l_ref[grp] = alpha * l_ref[grp] + jnp.sum(p, axis=-1, keepdims=True)
        pv = None
        for i, vb in enumerate(v_blocks):
            w = p[:, i * width:(i + 1) * width].astype(BF16)
            part = jnp.dot(w, vb, preferred_element_type=F32)
            pv = part if pv is None else pv + part
        acc_ref[grp] = alpha * acc_ref[grp] + pv
        m_ref[grp] = m_new

    row = lax.broadcasted_iota(jnp.int32, (gr, n_pg * cols), 0)
    col = lax.broadcasted_iota(jnp.int32, (gr, n_pg * cols), 1)
    own = (row // nt) == lax.rem(col, per)
    for grp in range(HEAD_GROUPS):
        qg = group_queries(grp)
        s = jnp.concatenate([_nt_dot(qg, group_rows(k_refs[i], grp)) for i in range(n_pg)], axis=1)
        s = jnp.where(own, s - ck_ref[0, grp:grp + 1, :], NEG_BIG)
        update(grp, s, [group_rows(v_refs[i], grp) for i in range(n_pg)], cols)

    @pl.when(step == pl.num_programs(1) - 1)
    def _():
        row = lax.broadcasted_iota(jnp.int32, (gr, V7X_LANES), 0)
        col = lax.broadcasted_iota(jnp.int32, (gr, V7X_LANES), 1)
        t_col = lax.rem(col, nt)
        visible = jnp.logical_and(jnp.logical_and((row // nt) == (col // nt), t_col <= lax.rem(row, nt)),
                                  jnp.logical_and(t_col < n_new, col < gr))
        pad = jnp.zeros((V7X_LANES - gr, dh), F32)
        for grp in range(HEAD_GROUPS):
            head_cols = [slice((grp + HEAD_GROUPS * j) * dh, (grp + HEAD_GROUPS * j + 1) * dh) for j in range(per)]
            k_new = jnp.concatenate([kn_ref[:, c] for c in head_cols] + [pad], axis=0).astype(BF16)
            v_new = jnp.concatenate([vn_ref[:, c] for c in head_cols] + [pad], axis=0).astype(BF16)
            s = _nt_dot(group_queries(grp), k_new)
            s = jnp.where(visible, s - cn_ref[0, grp:grp + 1, :], NEG_BIG)
            update(grp, s, [v_new], V7X_LANES)
            out = acc_ref[grp] / l_ref[grp]
            for j, c in enumerate(head_cols):
                o_ref[:, c] = out[j * nt:(j + 1) * nt].astype(o_ref.dtype)


def _fox_decode(page_table, proj, ck_past, c_new, cache_k, cache_v, layer, attn, *,
                n_prompt_rows, heads, dh, n_new, n_pg):
    bd, n_pages = page_table.shape
    page = cache_k.shape[2] // heads
    per = heads // HEAD_GROUPS
    gr = per * SAMPLE_ROWS
    hd = heads * dh
    sblk = n_prompt_rows // SAMPLE_ROWS

    def row_spec(col):
        return pl.BlockSpec((SAMPLE_ROWS, hd), lambda b, g, pt: (sblk + b, col))

    def page_spec(i):
        return pl.BlockSpec((None, None, page * heads, dh),
                            lambda b, g, pt: (layer, pt[b, g * n_pg + i], 0, 0))

    n_in = 5 + 2 * n_pg
    grid_spec = pltpu.PrefetchScalarGridSpec(
        num_scalar_prefetch=1,
        grid=(bd, n_pages // n_pg),
        in_specs=[row_spec(0), row_spec(1), row_spec(2),
                  pl.BlockSpec((1, HEAD_GROUPS, n_pg * page * per), lambda b, g, pt: (b, 0, g)),
                  pl.BlockSpec((1, HEAD_GROUPS, V7X_LANES), lambda b, g, pt: (b, 0, 0))]
        + [page_spec(i) for i in range(n_pg)] * 2
        + [pl.BlockSpec(memory_space=pl.ANY)],
        out_specs=pl.BlockSpec((SAMPLE_ROWS, hd), lambda b, g, pt: (sblk + b, 0)),
        scratch_shapes=[pltpu.VMEM((HEAD_GROUPS, gr, 1), F32),
                        pltpu.VMEM((HEAD_GROUPS, gr, 1), F32),
                        pltpu.VMEM((HEAD_GROUPS, gr, dh), F32)],
    )
    return pl.pallas_call(
        functools.partial(_fox_decode_kernel, n_pg=n_pg, heads=heads, dh=dh, page=page,
                          n_new=n_new, scale=dh ** -0.5),
        out_shape=jax.ShapeDtypeStruct(attn.shape, attn.dtype),
        grid_spec=grid_spec,
        input_output_aliases={n_in + 1: 0},
        compiler_params=_params("parallel", "arbitrary"),
        name="fox_decode",
    )(page_table, proj, proj, proj, ck_past, c_new,
      *([cache_k] * n_pg), *([cache_v] * n_pg), attn)


def _ret_tables(heads, chunk, t_real):
    lg = jnp.log1p(-jnp.exp2(-5.0 - jnp.arange(heads, dtype=F32)))
    n = jnp.arange(chunk, dtype=F32)
    diff = n[:, None] - n[None, :]
    causal = diff >= 0
    dmat = jnp.where(causal[None], jnp.exp(jnp.where(causal, diff, 0.0)[None] * lg[:, None, None]), 0.0)
    cross = jnp.exp((n[None, :] + 1.0) * lg[:, None])
    kdec = jnp.where(n[None, :] < t_real, jnp.exp((t_real - 1.0 - n)[None, :] * lg[:, None]), 0.0)
    full = jnp.broadcast_to(jnp.exp(t_real * lg)[:, None], (heads, chunk))
    dvec = jnp.stack([cross, kdec, full], axis=-1)
    return dmat.astype(F32), dvec.astype(F32)


def _rope_tables(positions, dk):
    half = dk // 2
    inv = ROPE_BASE ** (-jnp.arange(half, dtype=F32) / half)
    ang = positions.astype(F32)[:, None] * inv[None, :]
    return jnp.cos(ang), jnp.sin(ang)


def _ret_kernel(*refs, has_state, chunk, rows, n_chunks, dk):
    if has_state:
        q_ref, k_ref, v_ref, g_ref, cos_ref, sin_ref, dmat_ref, dvec_ref, s0_ref, y_ref, st_ref = refs
        st_ref[...] = s0_ref[...]
    else:
        q_ref, k_ref, v_ref, g_ref, cos_ref, sin_ref, dmat_ref, dvec_ref, y_ref, st_ref = refs
        st_ref[...] = jnp.zeros_like(st_ref)
    half = dk // 2

    def padded(x):
        x = x.astype(F32)
        if rows == chunk:
            return x
        return jnp.concatenate([x, jnp.zeros((chunk - rows, x.shape[1]), F32)], axis=0)

    dvec = dvec_ref[0]
    cross_d = dvec[:, 0:1]
    k_d = dvec[:, 1:2]
    full_d = dvec[0:1, 2:3]
    dmat = dmat_ref[0]

    def step(c, carry):
        sl = pl.ds(pl.multiple_of(c * rows, rows), rows)
        cos = padded(cos_ref[sl, :])
        sin = padded(sin_ref[sl, :])

        def rope(x):
            x1 = x[:, :half]
            x2 = x[:, half:]
            return jnp.concatenate([x1 * cos - x2 * sin, x1 * sin + x2 * cos], axis=1)

        q = rope(padded(q_ref[sl, :]))
        k = rope(padded(k_ref[sl, :])) * (dk ** -0.5)
        v = padded(v_ref[sl, :]).astype(BF16)
        qb = q.astype(BF16)
        s = _nt_dot(qb, k.astype(BF16)) * dmat
        state = st_ref[0, 0]
        y = jnp.dot(s.astype(BF16), v, preferred_element_type=F32)
        y = y + jnp.dot(qb, state.astype(BF16), preferred_element_type=F32) * cross_d
        kdec_t = (k * k_d).T.astype(BF16)
        st_ref[0, 0] = full_d * state + jnp.dot(kdec_t, v, preferred_element_type=F32)

        mu = jnp.mean(y, axis=-1, keepdims=True)
        yc = y - mu
        var = jnp.mean(yc * yc, axis=-1, keepdims=True)
        yn = yc * lax.rsqrt(var + GN_EPS)
        gate = padded(g_ref[sl, :])
        out = yn * (gate * jax.nn.sigmoid(gate))
        y_ref[sl, :] = out[:rows].astype(y_ref.dtype)
        return carry

    lax.fori_loop(0, n_chunks, step, 0, unroll=math.gcd(n_chunks, RET_UNROLL))


def _retention(proj, cos, sin, state0, layer, y_prev, *, row0, batch, n_chunks, rows, t_real,
               heads, dk, dv):
    chunk = max(rows, RET_CHUNK)
    dmat, dvec = _ret_tables(heads, chunk, t_real)
    seq_rows = n_chunks * rows
    r0 = row0 // seq_rows
    qk_w = heads * dk
    has_state = state0 is not None
    aliased = y_prev is not None

    def tok_spec(width, col0):
        cb = col0 // width
        return pl.BlockSpec((seq_rows, width), lambda b, h: (r0 + b, cb + h))

    in_specs = [
        tok_spec(dk, 0), tok_spec(dk, qk_w), tok_spec(dv, 2 * qk_w), tok_spec(dv, 2 * qk_w + heads * dv),
        pl.BlockSpec((seq_rows, dk // 2), lambda b, h: (0, 0)),
        pl.BlockSpec((seq_rows, dk // 2), lambda b, h: (0, 0)),
        pl.BlockSpec((1, chunk, chunk), lambda b, h: (h, 0, 0)),
        pl.BlockSpec((1, chunk, 3), lambda b, h: (h, 0, 0)),
    ]
    args = [proj, proj, proj, proj, cos, sin, dmat, dvec]
    if has_state:
        in_specs.append(pl.BlockSpec((None, 1, 1, dk, dv), lambda b, h: (layer, b, h, 0, 0)))
        args.append(state0)
    n_used = len(args)
    if aliased:
        in_specs.append(pl.BlockSpec(memory_space=pl.ANY))
        args.append(y_prev)
    n_in = len(args)

    def kernel(*refs):
        refs = refs[:n_used] + refs[n_in:]
        _ret_kernel(*refs, has_state=has_state, chunk=chunk, rows=rows, n_chunks=n_chunks, dk=dk)

    return pl.pallas_call(
        kernel,
        out_shape=[jax.ShapeDtypeStruct((proj.shape[0], heads * dv), BF16),
                   jax.ShapeDtypeStruct((batch, heads, dk, dv), F32)],
        grid=(batch, heads),
        in_specs=in_specs,
        out_specs=[
            pl.BlockSpec((seq_rows, dv), lambda b, h: (r0 + b, h)),
            pl.BlockSpec((1, 1, dk, dv), lambda b, h: (b, h, 0, 0)),
        ],
        input_output_aliases={n_in - 1: 0} if aliased else {},
        compiler_params=_params("parallel", "parallel"),
        name="retention",
    )(*args)


def kernel(x_prompt, x_sample, cache_fox_k, cache_fox_v, cache_fox_logf, state_ret, page_table, norm_ffn1, ffn1_w_gate, ffn1_w_up, ffn1_w_down, norm_mix, fox_w_in, fox_b_f, fox_w_out, ret_w_in, ret_w_out, norm_ffn2, ffn2_w_gate, ffn2_w_up, ffn2_w_down, norm_final):
    batch, seq, d = x_prompt.shape
    bd, n_new, _ = x_sample.shape
    depth = norm_ffn1.shape[0]
    f = ffn1_w_gate.shape[2]
    n_fox, pool, page, fox_heads, dh = cache_fox_k.shape
    ret_heads, dk, dv = state_ret.shape[2:]
    n_pages = page_table.shape[1]
    past = n_pages * page
    assert n_new <= SAMPLE_ROWS and fox_heads * dh == d and ret_heads * dk == d

    n_p = batch * seq
    n_s = bd * SAMPLE_ROWS
    rows = n_p + n_s
    tm = _tile(rows, 1100, V7X_BF16_SUBLANES)
    tf = _tile(f, V7X_MXU_DIM, V7X_LANES)
    tn = _tile(d, 512, V7X_LANES)
    tq = _tile(seq, 512, V7X_LANES)
    tr = _tile(n_p, 512, n_s)
    n_pg = _tile(n_pages, DECODE_PAGES_PER_STEP, 1)

    xs = jnp.pad(x_sample, ((0, 0), (0, SAMPLE_ROWS - n_new), (0, 0)))
    x = jnp.concatenate([x_prompt.reshape(n_p, d), xs.reshape(n_s, d)], axis=0)

    cache_k2 = cache_fox_k.reshape(n_fox, pool, page * fox_heads, dh)
    cache_v2 = cache_fox_v.reshape(n_fox, pool, page * fox_heads, dh)
    cache_logf_t = jnp.swapaxes(cache_fox_logf, 2, 3)
    fox_bias = jnp.pad(fox_b_f, ((0, 0), (0, V7X_LANES - fox_heads)))
    fox_w_in_t = jnp.swapaxes(fox_w_in, 1, 2)

    cos_p, sin_p = _rope_tables(jnp.arange(seq), dk)
    cos_s, sin_s = _rope_tables(past + jnp.arange(SAMPLE_ROWS), dk)

    def split_rows(a):
        w = a.shape[1]
        return (a[:n_p].reshape(batch, seq, w),
                a[n_p:].reshape(bd, SAMPLE_ROWS, w)[:, :n_new])

    kv_p = kv_s = None
    lp_l, ls_l, sp_l, ss_l = [], [], [], []
    for i in range(depth):
        x = _ffn(x, norm_ffn1[i][None], ffn1_w_gate, ffn1_w_up, ffn1_w_down, i, tm=tm, tf=tf)
        j = i // 2
        g_mix = norm_mix[i][None]
        if i % 2 == 0:
            proj = _norm_matmul(x, g_mix, fox_w_in_t, j, n=3 * d, tm=tm, tn=tn, out_dtype=F32,
                                w_transposed=True)
            logf = _forget_gate(x, g_mix, fox_w_in_t, j, fox_bias[j][None], row0=3 * d, heads=fox_heads, tm=tm)
            c_col = _prompt_cumsum(logf, batch=batch, seq=seq)
            attn = _fox_prompt(proj, c_col, rows=rows, batch=batch, seq=seq, heads=fox_heads, dh=dh, tq=tq,
                               hps=FOX_HEADS_PER_STEP)
            c_past, c_new = _decode_cumsum(page_table, cache_logf_t, j, logf,
                                           n_prompt_rows=n_p, heads=fox_heads)
            attn = _fox_decode(page_table, proj, c_past, c_new, cache_k2, cache_v2, j, attn,
                               n_prompt_rows=n_p, heads=fox_heads, dh=dh, n_new=n_new, n_pg=n_pg)
            x = _matmul_res(attn, fox_w_out, j, x, tm=tm, tn=tn)
            kv_p = _heads_layout(proj, kv_p, j, n_fox, row0=0, nrows=n_p, tr=tr, heads=fox_heads, dh=dh)
            kv_s = _heads_layout(proj, kv_s, j, n_fox, row0=n_p, nrows=n_s, tr=n_s, heads=fox_heads, dh=dh)
            lp, ls = split_rows(logf[:, :fox_heads])
            lp_l.append(lp)
            ls_l.append(ls)
        else:
            proj = _norm_matmul(x, g_mix, ret_w_in, j, n=ret_w_in.shape[2], tm=tm, tn=tn, out_dtype=BF16)
            y, sp = _retention(proj, cos_p, sin_p, None, j, None, row0=0, batch=batch,
                               n_chunks=seq // RET_CHUNK, rows=RET_CHUNK, t_real=RET_CHUNK,
                               heads=ret_heads, dk=dk, dv=dv)
            y, ss = _retention(proj, cos_s, sin_s, state_ret, j, y, row0=n_p, batch=bd,
                               n_chunks=1, rows=SAMPLE_ROWS, t_real=n_new,
                               heads=ret_heads, dk=dk, dv=dv)
            x = _matmul_res(y, ret_w_out, j, x, tm=tm, tn=tn)
            sp_l.append(sp)
            ss_l.append(ss)
        x = _ffn(x, norm_ffn2[i][None], ffn2_w_gate, ffn2_w_up, ffn2_w_down, i, tm=tm, tf=tf)

    g_fin = norm_final[None]
    y_prompt = _final_norm(x, g_fin, row0=0, nrows=n_p, tr=tr).reshape(batch, seq, d)
    y_sample = _final_norm(x, g_fin, row0=n_p, nrows=n_s, tr=n_s).reshape(bd, SAMPLE_ROWS, d)[:, :n_new]

    def heads_out(a, lead, n_tok):
        return a.reshape((n_fox,) + lead + (fox_heads, dh))[:, :, :n_tok]

    return (y_prompt, y_sample,
            heads_out(kv_p[0], (batch, seq), seq), heads_out(kv_p[1], (batch, seq), seq),
            jnp.stack(lp_l),
            heads_out(kv_s[0], (bd, SAMPLE_ROWS), n_new), heads_out(kv_s[1], (bd, SAMPLE_ROWS), n_new),
            jnp.stack(ls_l),
            jnp.stack(sp_l), jnp.stack(ss_l))
```

```python
import functools
import math

import jax
import jax.numpy as jnp
from jax import lax
from jax.experimental import pallas as pl
from jax.experimental.pallas import tpu as pltpu

F32 = jnp.float32
BF16 = jnp.bfloat16

NORM_EPS = 1e-6
GN_EPS = 1e-5
ROPE_BASE = 10000.0
NEG_BIG = -1e30
LOG2E = math.log2(math.e)

V7X_VMEM_BYTES = 64 * 1024 * 1024
V7X_LANES = 128
V7X_BF16_SUBLANES = 16
V7X_MXU_DIM = 256
VMEM_LIMIT = (V7X_VMEM_BYTES * 7) // 8

SAMPLE_ROWS = V7X_BF16_SUBLANES
RET_CHUNK = 128
RET_UNROLL = 8
N_DECAY_PIECES = 3
HEAD_GROUPS = 4
DECODE_PAGES_PER_STEP = 8
FOX_HEADS_PER_STEP = 2


def _tile(n, target, align):
    for t in range(min(n, target), 0, -1):
        if n % t == 0 and t % align == 0:
            return t
    raise ValueError(f"no tile for {n} (target {target}, align {align})")


def _params(*sem):
    return pltpu.CompilerParams(dimension_semantics=sem, vmem_limit_bytes=VMEM_LIMIT)


def _rms_norm(x, g):
    ms = jnp.mean(x * x, axis=-1, keepdims=True)
    return (x * lax.rsqrt(ms + NORM_EPS)) * g


def _nt_dot(a, b):
    return lax.dot_general(a, b, (((1,), (1,)), ((), ())), preferred_element_type=F32)


def _ffn_kernel(x_ref, g_ref, wg_ref, wu_ref, wd_ref, o_ref, h_ref, *, tf):
    @pl.when(pl.program_id(1) == 0)
    def _():
        x = x_ref[...]
        h_ref[...] = _rms_norm(x, g_ref[...]).astype(BF16)
        o_ref[...] = x

    wgu = jnp.concatenate([wg_ref[...].astype(BF16), wu_ref[...].astype(BF16)], axis=1)
    ab = jnp.dot(h_ref[...], wgu, preferred_element_type=F32)
    a = ab[:, :tf]
    b = ab[:, tf:]
    p = (a * jax.nn.sigmoid(a)) * (0.5 * b)
    o_ref[...] += jnp.dot(p.astype(BF16), wd_ref[...].astype(BF16), preferred_element_type=F32)


def _ffn(x, g, w_gate, w_up, w_down, layer, *, tm, tf):
    r, d = x.shape
    f = w_down.shape[1]
    return pl.pallas_call(
        functools.partial(_ffn_kernel, tf=tf),
        out_shape=jax.ShapeDtypeStruct((r, d), F32),
        grid=(r // tm, f // tf),
        in_specs=[
            pl.BlockSpec((tm, d), lambda i, j: (i, 0)),
            pl.BlockSpec((1, d), lambda i, j: (0, 0)),
            pl.BlockSpec((None, d, tf), lambda i, j: (layer, 0, j)),
            pl.BlockSpec((None, d, tf), lambda i, j: (layer, 0, j)),
            pl.BlockSpec((None, tf, d), lambda i, j: (layer, j, 0)),
        ],
        out_specs=pl.BlockSpec((tm, d), lambda i, j: (i, 0)),
        scratch_shapes=[pltpu.VMEM((tm, d), BF16)],
        compiler_params=_params("parallel", "arbitrary"),
        name="ffn",
    )(x, g, w_gate, w_up, w_down)


def _norm_matmul_kernel(x_ref, g_ref, w_ref, o_ref, h_ref, *, w_transposed):
    @pl.when(pl.program_id(1) == 0)
    def _():
        h_ref[...] = _rms_norm(x_ref[...], g_ref[...]).astype(BF16)

    w = w_ref[...].astype(BF16)
    if w_transposed:
        out = _nt_dot(h_ref[...], w)
    else:
        out = jnp.dot(h_ref[...], w, preferred_element_type=F32)
    o_ref[...] = out.astype(o_ref.dtype)


def _norm_matmul(x, g, w, layer, *, n, tm, tn, out_dtype, w_transposed=False):
    r, d = x.shape
    if w_transposed:
        w_spec = pl.BlockSpec((None, tn, d), lambda i, j: (layer, j, 0))
    else:
        w_spec = pl.BlockSpec((None, d, tn), lambda i, j: (layer, 0, j))
    return pl.pallas_call(
        functools.partial(_norm_matmul_kernel, w_transposed=w_transposed),
        out_shape=jax.ShapeDtypeStruct((r, n), out_dtype),
        grid=(r // tm, n // tn),
        in_specs=[
            pl.BlockSpec((tm, d), lambda i, j: (i, 0)),
            pl.BlockSpec((1, d), lambda i, j: (0, 0)),
            w_spec,
        ],
        out_specs=pl.BlockSpec((tm, tn), lambda i, j: (i, j)),
        scratch_shapes=[pltpu.VMEM((tm, d), BF16)],
        compiler_params=_params("parallel", "arbitrary"),
        name="norm_matmul",
    )(x, g, w)


def _forget_gate_kernel(x_ref, g_ref, w_ref, b_ref, o_ref, *, heads):
    h = _rms_norm(x_ref[...], g_ref[...]).astype(BF16)
    row = lax.broadcasted_iota(jnp.int32, w_ref.shape, 0)
    w = jnp.where(row < heads, w_ref[...], 0.0).astype(BF16)
    z = _nt_dot(h, w) + b_ref[...]
    o_ref[...] = -(jnp.maximum(-z, 0.0) + jnp.log1p(jnp.exp(-jnp.abs(z))))


def _forget_gate(x, g, w_t, layer, b_f, *, row0, heads, tm):
    r, d = x.shape
    n = V7X_LANES
    cb = row0 // n
    return pl.pallas_call(
        functools.partial(_forget_gate_kernel, heads=heads),
        out_shape=jax.ShapeDtypeStruct((r, n), F32),
        grid=(r // tm,),
        in_specs=[
            pl.BlockSpec((tm, d), lambda i: (i, 0)),
            pl.BlockSpec((1, d), lambda i: (0, 0)),
            pl.BlockSpec((None, n, d), lambda i: (layer, cb, 0)),
            pl.BlockSpec((1, n), lambda i: (0, 0)),
        ],
        out_specs=pl.BlockSpec((tm, n), lambda i: (i, 0)),
        compiler_params=_params("parallel"),
        name="forget_gate",
    )(x, g, w_t, b_f)


def _matmul_res_kernel(a_ref, w_ref, r_ref, o_ref):
    w = w_ref[...].astype(BF16)
    o_ref[...] = r_ref[...] + jnp.dot(a_ref[...], w, preferred_element_type=F32)


def _matmul_res(a, w, layer, res, *, tm, tn):
    r, k = a.shape
    n = w.shape[2]
    return pl.pallas_call(
        _matmul_res_kernel,
        out_shape=jax.ShapeDtypeStruct((r, n), F32),
        grid=(r // tm, n // tn),
        in_specs=[
            pl.BlockSpec((tm, k), lambda i, j: (i, 0)),
            pl.BlockSpec((None, k, tn), lambda i, j: (layer, 0, j)),
            pl.BlockSpec((tm, tn), lambda i, j: (i, j)),
        ],
        out_specs=pl.BlockSpec((tm, tn), lambda i, j: (i, j)),
        compiler_params=_params("parallel", "arbitrary"),
        name="matmul_res",
    )(a, w, res)


def _final_norm_kernel(x_ref, g_ref, o_ref):
    o_ref[...] = _rms_norm(x_ref[...], g_ref[...])


def _final_norm(x, g, *, row0, nrows, tr):
    d = x.shape[1]
    b0 = row0 // tr
    return pl.pallas_call(
        _final_norm_kernel,
        out_shape=jax.ShapeDtypeStruct((nrows, d), F32),
        grid=(nrows // tr,),
        in_specs=[pl.BlockSpec((tr, d), lambda i: (b0 + i, 0)), pl.BlockSpec((1, d), lambda i: (0, 0))],
        out_specs=pl.BlockSpec((tr, d), lambda i: (i, 0)),
        compiler_params=_params("parallel"),
        name="final_norm",
    )(x, g)


def _heads_layout_kernel(*refs, heads, dh, aliased):
    k_ref, v_ref = refs[:2]
    ko_ref, vo_ref = refs[2 + 2 * aliased:]
    tr = k_ref.shape[0]
    for src, dst in ((k_ref, ko_ref), (v_ref, vo_ref)):
        for h in range(heads):
            dst[pl.ds(h, tr, stride=heads), :] = src[:, h * dh:(h + 1) * dh]


def _heads_layout(proj, prev, layer, n_layers, *, row0, nrows, tr, heads, dh):
    d = heads * dh
    b0 = row0 // tr
    aliased = prev is not None
    shape = jax.ShapeDtypeStruct((n_layers, nrows * heads, dh), F32)
    in_specs = [pl.BlockSpec((tr, d), lambda i: (b0 + i, 1)),
                pl.BlockSpec((tr, d), lambda i: (b0 + i, 2))]
    args = [proj, proj]
    if aliased:
        in_specs += [pl.BlockSpec(memory_space=pl.ANY)] * 2
        args += list(prev)
    out_spec = pl.BlockSpec((None, tr * heads, dh), lambda i: (layer, i, 0))
    return pl.pallas_call(
        functools.partial(_heads_layout_kernel, heads=heads, dh=dh, aliased=aliased),
        out_shape=[shape, shape],
        grid=(nrows // tr,),
        in_specs=in_specs,
        out_specs=[out_spec, out_spec],
        input_output_aliases={2: 0, 3: 1} if aliased else {},
        compiler_params=_params("parallel"),
        name="heads_layout",
    )(*args)


def _split_bf16(x):
    pieces = []
    rem = x
    for _ in range(N_DECAY_PIECES):
        p = rem.astype(BF16)
        pieces.append(p)
        rem = rem - p.astype(F32)
    return pieces


def _tri_ones(n, lower):
    row = lax.broadcasted_iota(jnp.int32, (n, n), 0)
    col = lax.broadcasted_iota(jnp.int32, (n, n), 1)
    keep = (col <= row) if lower else (row <= col)
    return jnp.where(keep, 1.0, 0.0).astype(BF16)


def _prompt_cumsum_kernel(lf_ref, c_ref, *, blk):
    t = lf_ref.shape[0]
    lower = _tri_ones(blk, lower=True)
    carry = jnp.zeros((1, lf_ref.shape[1]), F32)
    for i in range(t // blk):
        local = carry
        for piece in _split_bf16(lf_ref[i * blk:(i + 1) * blk, :]):
            local = local + jnp.dot(lower, piece, preferred_element_type=F32)
        c_ref[i * blk:(i + 1) * blk, :] = local
        carry = local[blk - 1:blk, :]


def _prompt_cumsum(logf, *, batch, seq):
    w = logf.shape[1]
    return pl.pallas_call(
        functools.partial(_prompt_cumsum_kernel, blk=V7X_LANES),
        out_shape=jax.ShapeDtypeStruct((batch * seq, w), F32),
        grid=(batch,),
        in_specs=[pl.BlockSpec((seq, w), lambda b: (b, 0))],
        out_specs=pl.BlockSpec((seq, w), lambda b: (b, 0)),
        compiler_params=_params("parallel"),
        name="prompt_cumsum",
    )(logf)


def _lane_cumsum_block(xt, upper):
    out = None
    for piece in _split_bf16(xt):
        part = jnp.dot(piece, upper, preferred_element_type=F32)
        out = part if out is None else out + part
    return out


def _regroup(c, spread, valid, heads):
    y = None
    for piece in _split_bf16(c):
        part = jnp.dot(piece, spread, preferred_element_type=F32)
        y = part if y is None else y + part
    out = []
    for i in range(c.shape[0] // heads):
        blk = y[i * heads:(i + 1) * heads]
        rows = [jnp.sum(jnp.where(v, blk, 0.0), axis=0, keepdims=True) for v in valid]
        out.append(jnp.concatenate(rows, axis=0))
    return out


def _decode_cumsum_kernel(pt_ref, *refs, n_pg, heads, page):
    page_refs = refs[:n_pg]
    new_ref = refs[n_pg]
    c_ref, cn_ref, carry_ref = refs[n_pg + 1:]
    g = pl.program_id(1)
    per = heads // HEAD_GROUPS
    nt = new_ref.shape[0]
    nr = n_pg * heads

    @pl.when(g == 0)
    def _():
        carry_ref[...] = jnp.zeros_like(carry_ref)

    def masks(m, head_slot):
        head = lax.broadcasted_iota(jnp.int32, (heads, m), 0)
        col = lax.broadcasted_iota(jnp.int32, (heads, m), 1)
        owns = (head // HEAD_GROUPS) == head_slot(col)
        return [jnp.logical_and(lax.rem(head, HEAD_GROUPS) == grp, owns) for grp in range(HEAD_GROUPS)]

    m_past = page * per
    t_idx = lax.broadcasted_iota(jnp.int32, (page, m_past), 0)
    c_idx = lax.broadcasted_iota(jnp.int32, (page, m_past), 1)
    spread = jnp.where(c_idx // per == t_idx, 1.0, 0.0).astype(BF16)
    upper = _tri_ones(page, lower=False)

    x = jnp.concatenate([page_refs[i][...] for i in range(n_pg)], axis=0)
    within = _lane_cumsum_block(x, upper)
    totals = jnp.broadcast_to(within[:, page - 1:page], (nr, page))
    r_idx = lax.broadcasted_iota(jnp.int32, (nr, nr), 0)
    q_idx = lax.broadcasted_iota(jnp.int32, (nr, nr), 1)
    earlier_page = jnp.logical_and(lax.rem(r_idx, heads) == lax.rem(q_idx, heads), q_idx < r_idx)
    earlier_page = jnp.where(earlier_page, 1.0, 0.0).astype(BF16)
    offset = None
    for piece in _split_bf16(totals):
        part = jnp.dot(earlier_page, piece, preferred_element_type=F32)
        offset = part if offset is None else offset + part
    carry = carry_ref[...]
    local = within + offset + jnp.concatenate([carry] * n_pg, axis=0)
    carry = local[nr - heads:, page - 1:page]
    carry_ref[...] = carry
    grouped = _regroup(local, spread, masks(m_past, lambda col: lax.rem(col, per)), heads)
    for i in range(n_pg):
        c_ref[0, :, i * m_past:(i + 1) * m_past] = grouped[i]

    @pl.when(g == pl.num_programs(1) - 1)
    def _():
        xn = new_ref[...]
        xn = jnp.concatenate([xn, jnp.zeros((V7X_LANES - nt, xn.shape[1]), F32)], axis=0)
        new = _lane_cumsum_block(xn.T[:heads, :], upper) + carry
        t_new = lax.broadcasted_iota(jnp.int32, (V7X_LANES, V7X_LANES), 0)
        c_new = lax.broadcasted_iota(jnp.int32, (V7X_LANES, V7X_LANES), 1)
        hit = jnp.logical_and(lax.rem(c_new, nt) == t_new, c_new < per * nt)
        cn_ref[0] = _regroup(new, jnp.where(hit, 1.0, 0.0).astype(BF16),
                             masks(V7X_LANES, lambda col: col // nt), heads)[0]


def _decode_cumsum(page_table, cache_logf, layer, logf, *, n_prompt_rows, heads):
    bd, n_pages = page_table.shape
    page = cache_logf.shape[3]
    per = heads // HEAD_GROUPS
    assert heads % HEAD_GROUPS == 0 and per * SAMPLE_ROWS <= V7X_LANES and page == V7X_LANES
    n_pg = _tile(n_pages, 16, 1)
    past = n_pages * page

    def page_spec(i):
        return pl.BlockSpec((None, None, heads, page),
                            lambda b, g, pt: (layer, pt[b, g * n_pg + i], 0, 0))

    new_block = n_prompt_rows // SAMPLE_ROWS
    grid_spec = pltpu.PrefetchScalarGridSpec(
        num_scalar_prefetch=1,
        grid=(bd, n_pages // n_pg),
        in_specs=[page_spec(i) for i in range(n_pg)]
        + [pl.BlockSpec((SAMPLE_ROWS, logf.shape[1]), lambda b, g, pt: (new_block + b, 0))],
        out_specs=[
            pl.BlockSpec((1, HEAD_GROUPS, n_pg * page * per), lambda b, g, pt: (b, 0, g)),
            pl.BlockSpec((1, HEAD_GROUPS, V7X_LANES), lambda b, g, pt: (b, 0, 0)),
        ],
        scratch_shapes=[pltpu.VMEM((heads, 1), F32)],
    )
    return pl.pallas_call(
        functools.partial(_decode_cumsum_kernel, n_pg=n_pg, heads=heads, page=page),
        out_shape=[jax.ShapeDtypeStruct((bd, HEAD_GROUPS, past * per), F32),
                   jax.ShapeDtypeStruct((bd, HEAD_GROUPS, V7X_LANES), F32)],
        grid_spec=grid_spec,
        compiler_params=_params("parallel", "arbitrary"),
        name="decode_cumsum",
    )(page_table, *([cache_logf] * n_pg), logf)


def _fox_prompt_kernel(q_ref, k_ref, v_ref, c_ref, o_ref, kaug_ref, vt_ref, acc_ref, *, tq, dh, scale):
    qi = pl.program_id(2)
    hps, nk = vt_ref.shape[0], vt_ref.shape[1]
    lanes = c_ref.shape[1]

    @pl.when(qi == 0)
    def _():
        row = lax.broadcasted_iota(jnp.int32, (lanes, dh), 0)
        col = lax.broadcasted_iota(jnp.int32, (lanes, dh), 1)
        for j in range(nk):
            sl = slice(j * tq, (j + 1) * tq)
            pieces = _split_bf16(c_ref[sl, :] * LOG2E)
            for u in range(hps):
                head = pl.program_id(1) * hps + u
                hc = slice(u * dh, (u + 1) * dh)
                aug = jnp.zeros((tq, dh), F32)
                for i, piece in enumerate(pieces):
                    place = jnp.where(jnp.logical_and(row == head, col == i), 1.0, 0.0).astype(BF16)
                    aug = aug + jnp.dot(piece, place, preferred_element_type=F32)
                kaug_ref[u, j, :, :dh] = k_ref[sl, hc].astype(BF16)
                kaug_ref[u, j, :, dh:] = aug.astype(BF16)
                vt_ref[u, j] = v_ref[sl, hc].T.astype(BF16)

    lane = lax.broadcasted_iota(jnp.int32, (tq, dh), 1)
    minus_ones = jnp.where(lane < N_DECAY_PIECES, -1.0, 0.0).astype(BF16)
    q_aug = [jnp.concatenate([(q_ref[:, u * dh:(u + 1) * dh] * (scale * LOG2E)).astype(BF16), minus_ones],
                             axis=1) for u in range(hps)]

    def absorb(kj, m, l, u, diagonal):
        s = _nt_dot(kaug_ref[u, kj], q_aug[u])
        if diagonal:
            key = lax.broadcasted_iota(jnp.int32, (tq, tq), 0)
            qry = lax.broadcasted_iota(jnp.int32, (tq, tq), 1)
            s = jnp.where(key <= qry, s, NEG_BIG)
        m_new = jnp.maximum(m, jnp.max(s, axis=0, keepdims=True))
        alpha = jnp.exp2(m - m_new)
        p = jnp.exp2(s - m_new)
        l = alpha * l + jnp.sum(p, axis=0, keepdims=True)
        pv = jnp.dot(vt_ref[u, kj], p.astype(BF16), preferred_element_type=F32)
        acc_ref[u] = pv if kj == 0 else alpha * acc_ref[u] + pv
        return m_new, l

    for last in range(nk):
        @pl.when(qi == last)
        def _(last=last):
            for u in range(hps):
                m = jnp.full((1, tq), NEG_BIG, F32)
                l = jnp.zeros((1, tq), F32)
                for kj in range(last + 1):
                    m, l = absorb(kj, m, l, u, kj == last)
                o_ref[:, u * dh:(u + 1) * dh] = (acc_ref[u] / l).T.astype(o_ref.dtype)


def _fox_prompt(proj, c_col, *, rows, batch, seq, heads, dh, tq, hps):
    nq = seq // tq
    hb = heads // hps
    w = hps * dh
    return pl.pallas_call(
        functools.partial(_fox_prompt_kernel, tq=tq, dh=dh, scale=dh ** -0.5),
        out_shape=jax.ShapeDtypeStruct((rows, heads * dh), BF16),
        grid=(batch, hb, nq),
        in_specs=[
            pl.BlockSpec((tq, w), lambda b, h, i: (b * nq + i, h)),
            pl.BlockSpec((seq, w), lambda b, h, i: (b, hb + h)),
            pl.BlockSpec((seq, w), lambda b, h, i: (b, 2 * hb + h)),
            pl.BlockSpec((seq, c_col.shape[1]), lambda b, h, i: (b, 0)),
        ],
        out_specs=pl.BlockSpec((tq, w), lambda b, h, i: (b * nq + i, h)),
        scratch_shapes=[pltpu.VMEM((hps, nq, tq, 2 * dh), BF16),
                        pltpu.VMEM((hps, nq, dh, tq), BF16),
                        pltpu.VMEM((hps, dh, tq), F32)],
        compiler_params=_params("parallel", "parallel", "arbitrary"),
        name="fox_prompt",
    )(proj, proj, proj, c_col)


def _fox_decode_kernel(pt_ref, q_ref, kn_ref, vn_ref, ck_ref, cn_ref, *refs,
                       n_pg, heads, dh, page, n_new, scale):
    k_refs = refs[:n_pg]
    v_refs = refs[n_pg:2 * n_pg]
    o_ref, m_ref, l_ref, acc_ref = refs[2 * n_pg + 1:]
    step = pl.program_id(1)
    nt = q_ref.shape[0]
    per = heads // HEAD_GROUPS
    gr = per * nt
    cols = page * per

    @pl.when(step == 0)
    def _():
        m_ref[...] = jnp.full_like(m_ref, NEG_BIG)
        l_ref[...] = jnp.zeros_like(l_ref)
        acc_ref[...] = jnp.zeros_like(acc_ref)

    def group_rows(ref, grp):
        return ref[pl.ds(grp, cols, stride=HEAD_GROUPS), :].astype(BF16)

    def group_queries(grp):
        parts = [q_ref[:, (grp + HEAD_GROUPS * j) * dh:(grp + HEAD_GROUPS * j + 1) * dh] for j in range(per)]
        return (jnp.concatenate(parts, axis=0) * scale).astype(BF16)

    def update(grp, s, v_blocks, width):
        m_old = m_ref[grp]
        m_new = jnp.maximum(m_old, jnp.max(s, axis=-1, keepdims=True))
        alpha = jnp.exp(m_old - m_new)
        p = jnp.exp(s - m_new)
        l_ref[grp] = alpha * l_ref[grp] + jnp.sum(p, axis=-1, keepdims=True)
        pv = None
        for i, vb in enumerate(v_blocks):
            w = p[:, i * width:(i + 1) * width].astype(BF16)
            part = jnp.dot(w, vb, preferred_element_type=F32)
            pv = part if pv is None else pv + part
        acc_ref[grp] = alpha * acc_ref[grp] + pv
        m_ref[grp] = m_new

    row = lax.broadcasted_iota(jnp.int32, (gr, n_pg * cols), 0)
    col = lax.broadcasted_iota(jnp.int32, (gr, n_pg * cols), 1)
    own = (row // nt) == lax.rem(col, per)
    for grp in range(HEAD_GROUPS):
        qg = group_queries(grp)
        s = jnp.concatenate([_nt_dot(qg, group_rows(k_refs[i], grp)) for i in range(n_pg)], axis=1)
        s = jnp.where(own, s - ck_ref[0, grp:grp + 1, :], NEG_BIG)
        update(grp, s, [group_rows(v_refs[i], grp) for i in range(n_pg)], cols)

    @pl.when(step == pl.num_programs(1) - 1)
    def _():
        row = lax.broadcasted_iota(jnp.int32, (gr, V7X_LANES), 0)
        col = lax.broadcasted_iota(jnp.int32, (gr, V7X_LANES), 1)
        t_col = lax.rem(col, nt)
        visible = jnp.logical_and(jnp.logical_and((row // nt) == (col // nt), t_col <= lax.rem(row, nt)),
                                  jnp.logical_and(t_col < n_new, col < gr))
        pad = jnp.zeros((V7X_LANES - gr, dh), F32)
        for grp in range(HEAD_GROUPS):
            head_cols = [slice((grp + HEAD_GROUPS * j) * dh, (grp + HEAD_GROUPS * j + 1) * dh) for j in range(per)]
            k_new = jnp.concatenate([kn_ref[:, c] for c in head_cols] + [pad], axis=0).astype(BF16)
            v_new = jnp.concatenate([vn_ref[:, c] for c in head_cols] + [pad], axis=0).astype(BF16)
            s = _nt_dot(group_queries(grp), k_new)
            s = jnp.where(visible, s - cn_ref[0, grp:grp + 1, :], NEG_BIG)
            update(grp, s, [v_new], V7X_LANES)
            out = acc_ref[grp] / l_ref[grp]
            for j, c in enumerate(head_cols):
                o_ref[:, c] = out[j * nt:(j + 1) * nt].astype(o_ref.dtype)


def _fox_decode(page_table, proj, ck_past, c_new, cache_k, cache_v, layer, attn, *,
                n_prompt_rows, heads, dh, n_new, n_pg):
    bd, n_pages = page_table.shape
    page = cache_k.shape[2] // heads
    per = heads // HEAD_GROUPS
    gr = per * SAMPLE_ROWS
    hd = heads * dh
    sblk = n_prompt_rows // SAMPLE_ROWS

    def row_spec(col):
        return pl.BlockSpec((SAMPLE_ROWS, hd), lambda b, g, pt: (sblk + b, col))

    def page_spec(i):
        return pl.BlockSpec((None, None, page * heads, dh),
                            lambda b, g, pt: (layer, pt[b, g * n_pg + i], 0, 0))

    n_in = 5 + 2 * n_pg
    grid_spec = pltpu.PrefetchScalarGridSpec(
        num_scalar_prefetch=1,
        grid=(bd, n_pages // n_pg),
        in_specs=[row_spec(0), row_spec(1), row_spec(2),
                  pl.BlockSpec((1, HEAD_GROUPS, n_pg * page * per), lambda b, g, pt: (b, 0, g)),
                  pl.BlockSpec((1, HEAD_GROUPS, V7X_LANES), lambda b, g, pt: (b, 0, 0))]
        + [page_spec(i) for i in range(n_pg)] * 2
        + [pl.BlockSpec(memory_space=pl.ANY)],
        out_specs=pl.BlockSpec((SAMPLE_ROWS, hd), lambda b, g, pt: (sblk + b, 0)),
        scratch_shapes=[pltpu.VMEM((HEAD_GROUPS, gr, 1), F32),
                        pltpu.VMEM((HEAD_GROUPS, gr, 1), F32),
                        pltpu.VMEM((HEAD_GROUPS, gr, dh), F32)],
    )
    return pl.pallas_call(
        functools.partial(_fox_decode_kernel, n_pg=n_pg, heads=heads, dh=dh, page=page,
                          n_new=n_new, scale=dh ** -0.5),
        out_shape=jax.ShapeDtypeStruct(attn.shape, attn.dtype),
        grid_spec=grid_spec,
        input_output_aliases={n_in + 1: 0},
        compiler_params=_params("parallel", "arbitrary"),
        name="fox_decode",
    )(page_table, proj, proj, proj, ck_past, c_new,
      *([cache_k] * n_pg), *([cache_v] * n_pg), attn)


def _ret_tables(heads, chunk, t_real):
    lg = jnp.log1p(-jnp.exp2(-5.0 - jnp.arange(heads, dtype=F32)))
    n = jnp.arange(chunk, dtype=F32)
    diff = n[:, None] - n[None, :]
    causal = diff >= 0
    dmat = jnp.where(causal[None], jnp.exp(jnp.where(causal, diff, 0.0)[None] * lg[:, None, None]), 0.0)
    cross = jnp.exp((n[None, :] + 1.0) * lg[:, None])
    kdec = jnp.where(n[None, :] < t_real, jnp.exp((t_real - 1.0 - n)[None, :] * lg[:, None]), 0.0)
    full = jnp.broadcast_to(jnp.exp(t_real * lg)[:, None], (heads, chunk))
    dvec = jnp.stack([cross, kdec, full], axis=-1)
    return dmat.astype(F32), dvec.astype(F32)


def _rope_tables(positions, dk):
    half = dk // 2
    inv = ROPE_BASE ** (-jnp.arange(half, dtype=F32) / half)
    ang = positions.astype(F32)[:, None] * inv[None, :]
    return jnp.cos(ang), jnp.sin(ang)


def _ret_kernel(*refs, has_state, chunk, rows, n_chunks, dk):
    if has_state:
        q_ref, k_ref, v_ref, g_ref, cos_ref, sin_ref, dmat_ref, dvec_ref, s0_ref, y_ref, st_ref = refs
        st_ref[...] = s0_ref[...]
    else:
        q_ref, k_ref, v_ref, g_ref, cos_ref, sin_ref, dmat_ref, dvec_ref, y_ref, st_ref = refs
        st_ref[...] = jnp.zeros_like(st_ref)
    half = dk // 2

    def padded(x):
        x = x.astype(F32)
        if rows == chunk:
            return x
        return jnp.concatenate([x, jnp.zeros((chunk - rows, x.shape[1]), F32)], axis=0)

    dvec = dvec_ref[0]
    cross_d = dvec[:, 0:1]
    k_d = dvec[:, 1:2]
    full_d = dvec[0:1, 2:3]
    dmat = dmat_ref[0]

    def step(c, carry):
        sl = pl.ds(pl.multiple_of(c * rows, rows), rows)
        cos = padded(cos_ref[sl, :])
        sin = padded(sin_ref[sl, :])

        def rope(x):
            x1 = x[:, :half]
            x2 = x[:, half:]
            return jnp.concatenate([x1 * cos - x2 * sin, x1 * sin + x2 * cos], axis=1)

        q = rope(padded(q_ref[sl, :]))
        k = rope(padded(k_ref[sl, :])) * (dk ** -0.5)
        v = padded(v_ref[sl, :]).astype(BF16)
        qb = q.astype(BF16)
        s = _nt_dot(qb, k.astype(BF16)) * dmat
        state = st_ref[0, 0]
        y = jnp.dot(s.astype(BF16), v, preferred_element_type=F32)
        y = y + jnp.dot(qb, state.astype(BF16), preferred_element_type=F32) * cross_d
        kdec_t = (k * k_d).T.astype(BF16)
        st_ref[0, 0] = full_d * state + jnp.dot(kdec_t, v, preferred_element_type=F32)

        mu = jnp.mean(y, axis=-1, keepdims=True)
        yc = y - mu
        var = jnp.mean(yc * yc, axis=-1, keepdims=True)
        yn = yc * lax.rsqrt(var + GN_EPS)
        gate = padded(g_ref[sl, :])
        out = yn * (gate * jax.nn.sigmoid(gate))
        y_ref[sl, :] = out[:rows].astype(y_ref.dtype)
        return carry

    lax.fori_loop(0, n_chunks, step, 0, unroll=math.gcd(n_chunks, RET_UNROLL))


def _retention(proj, cos, sin, state0, layer, y_prev, *, row0, batch, n_chunks, rows, t_real,
               heads, dk, dv):
    chunk = max(rows, RET_CHUNK)
    dmat, dvec = _ret_tables(heads, chunk, t_real)
    seq_rows = n_chunks * rows
    r0 = row0 // seq_rows
    qk_w = heads * dk
    has_state = state0 is not None
    aliased = y_prev is not None

    def tok_spec(width, col0):
        cb = col0 // width
        return pl.BlockSpec((seq_rows, width), lambda b, h: (r0 + b, cb + h))

    in_specs = [
        tok_spec(dk, 0), tok_spec(dk, qk_w), tok_spec(dv, 2 * qk_w), tok_spec(dv, 2 * qk_w + heads * dv),
        pl.BlockSpec((seq_rows, dk // 2), lambda b, h: (0, 0)),
        pl.BlockSpec((seq_rows, dk // 2), lambda b, h: (0, 0)),
        pl.BlockSpec((1, chunk, chunk), lambda b, h: (h, 0, 0)),
        pl.BlockSpec((1, chunk, 3), lambda b, h: (h, 0, 0)),
    ]
    args = [proj, proj, proj, proj, cos, sin, dmat, dvec]
    if has_state:
        in_specs.append(pl.BlockSpec((None, 1, 1, dk, dv), lambda b, h: (layer, b, h, 0, 0)))
        args.append(state0)
    n_used = len(args)
    if aliased:
        in_specs.append(pl.BlockSpec(memory_space=pl.ANY))
        args.append(y_prev)
    n_in = len(args)

    def kernel(*refs):
        refs = refs[:n_used] + refs[n_in:]
        _ret_kernel(*refs, has_state=has_state, chunk=chunk, rows=rows, n_chunks=n_chunks, dk=dk)

    return pl.pallas_call(
        kernel,
        out_shape=[jax.ShapeDtypeStruct((proj.shape[0], heads * dv), BF16),
                   jax.ShapeDtypeStruct((batch, heads, dk, dv), F32)],
        grid=(batch, heads),
        in_specs=in_specs,
        out_specs=[
            pl.BlockSpec((seq_rows, dv), lambda b, h: (r0 + b, h)),
            pl.BlockSpec((1, 1, dk, dv), lambda b, h: (b, h, 0, 0)),
        ],
        input_output_aliases={n_in - 1: 0} if aliased else {},
        compiler_params=_params("parallel", "parallel"),
        name="retention",
    )(*args)


def kernel(x_prompt, x_sample, cache_fox_k, cache_fox_v, cache_fox_logf, state_ret, page_table, norm_ffn1, ffn1_w_gate, ffn1_w_up, ffn1_w_down, norm_mix, fox_w_in, fox_b_f, fox_w_out, ret_w_in, ret_w_out, norm_ffn2, ffn2_w_gate, ffn2_w_up, ffn2_w_down, norm_final):
    batch, seq, d = x_prompt.shape
    bd, n_new, _ = x_sample.shape
    depth = norm_ffn1.shape[0]
    f = ffn1_w_gate.shape[2]
    n_fox, pool, page, fox_heads, dh = cache_fox_k.shape
    ret_heads, dk, dv = state_ret.shape[2:]
    n_pages = page_table.shape[1]
    past = n_pages * page
    assert n_new <= SAMPLE_ROWS and fox_heads * dh == d and ret_heads * dk == d

    n_p = batch * seq
    n_s = bd * SAMPLE_ROWS
    rows = n_p + n_s
    tm = _tile(rows, 1100, V7X_BF16_SUBLANES)
    tf = _tile(f, V7X_MXU_DIM, V7X_LANES)
    tn = _tile(d, 1024, V7X_LANES)
    tn_deep = _tile(d, 512, V7X_LANES)
    tq = _tile(seq, 1024, V7X_LANES)
    tr = _tile(n_p, 512, n_s)
    n_pg = _tile(n_pages, DECODE_PAGES_PER_STEP, 1)

    xs = jnp.pad(x_sample, ((0, 0), (0, SAMPLE_ROWS - n_new), (0, 0)))
    x = jnp.concatenate([x_prompt.reshape(n_p, d), xs.reshape(n_s, d)], axis=0)

    cache_k2 = cache_fox_k.reshape(n_fox, pool, page * fox_heads, dh)
    cache_v2 = cache_fox_v.reshape(n_fox, pool, page * fox_heads, dh)
    cache_logf_t = jnp.swapaxes(cache_fox_logf, 2, 3)
    fox_bias = jnp.pad(fox_b_f, ((0, 0), (0, V7X_LANES - fox_heads)))
    fox_w_in_t = jnp.swapaxes(fox_w_in, 1, 2)

    cos_p, sin_p = _rope_tables(jnp.arange(seq), dk)
    cos_s, sin_s = _rope_tables(past + jnp.arange(SAMPLE_ROWS), dk)

    def split_rows(a):
        w = a.shape[1]
        return (a[:n_p].reshape(batch, seq, w),
                a[n_p:].reshape(bd, SAMPLE_ROWS, w)[:, :n_new])

    kv_p = kv_s = None
    lp_l, ls_l, sp_l, ss_l = [], [], [], []
    for i in range(depth):
        x = _ffn(x, norm_ffn1[i][None], ffn1_w_gate, ffn1_w_up, ffn1_w_down, i, tm=tm, tf=tf)
        j = i // 2
        g_mix = norm_mix[i][None]
        if i % 2 == 0:
            proj = _norm_matmul(x, g_mix, fox_w_in_t, j, n=3 * d, tm=tm, tn=tn, out_dtype=F32,
                                w_transposed=True)
            logf = _forget_gate(x, g_mix, fox_w_in_t, j, fox_bias[j][None], row0=3 * d, heads=fox_heads, tm=tm)
            c_col = _prompt_cumsum(logf, batch=batch, seq=seq)
            attn = _fox_prompt(proj, c_col, rows=rows, batch=batch, seq=seq, heads=fox_heads, dh=dh, tq=tq,
                               hps=FOX_HEADS_PER_STEP)
            c_past, c_new = _decode_cumsum(page_table, cache_logf_t, j, logf,
                                           n_prompt_rows=n_p, heads=fox_heads)
            attn = _fox_decode(page_table, proj, c_past, c_new, cache_k2, cache_v2, j, attn,
                               n_prompt_rows=n_p, heads=fox_heads, dh=dh, n_new=n_new, n_pg=n_pg)
            x = _matmul_res(attn, fox_w_out, j, x, tm=tm, tn=tn)
            kv_p = _heads_layout(proj, kv_p, j, n_fox, row0=0, nrows=n_p, tr=tr, heads=fox_heads, dh=dh)
            kv_s = _heads_layout(proj, kv_s, j, n_fox, row0=n_p, nrows=n_s, tr=n_s, heads=fox_heads, dh=dh)
            lp, ls = split_rows(logf[:, :fox_heads])
            lp_l.append(lp)
            ls_l.append(ls)
        else:
            proj = _norm_matmul(x, g_mix, ret_w_in, j, n=ret_w_in.shape[2], tm=tm, tn=tn, out_dtype=BF16)
            y, sp = _retention(proj, cos_p, sin_p, None, j, None, row0=0, batch=batch,
                               n_chunks=seq // RET_CHUNK, rows=RET_CHUNK, t_real=RET_CHUNK,
                               heads=ret_heads, dk=dk, dv=dv)
            y, ss = _retention(proj, cos_s, sin_s, state_ret, j, y, row0=n_p, batch=bd,
                               n_chunks=1, rows=SAMPLE_ROWS, t_real=n_new,
                               heads=ret_heads, dk=dk, dv=dv)
            x = _matmul_res(y, ret_w_out, j, x, tm=tm, tn=tn_deep)
            sp_l.append(sp)
            ss_l.append(ss)
        x = _ffn(x, norm_ffn2[i][None], ffn2_w_gate, ffn2_w_up, ffn2_w_down, i, tm=tm, tf=tf)

    g_fin = norm_final[None]
    y_prompt = _final_norm(x, g_fin, row0=0, nrows=n_p, tr=tr).reshape(batch, seq, d)
    y_sample = _final_norm(x, g_fin, row0=n_p, nrows=n_s, tr=n_s).reshape(bd, SAMPLE_ROWS, d)[:, :n_new]

    def heads_out(a, lead, n_tok):
        return a.reshape((n_fox,) + lead + (fox_heads, dh))[:, :, :n_tok]

    return (y_prompt, y_sample,
            heads_out(kv_p[0], (batch, seq), seq), heads_out(kv_p[1], (batch, seq), seq),
            jnp.stack(lp_l),
            heads_out(kv_s[0], (bd, SAMPLE_ROWS), n_new), heads_out(kv_s[1], (bd, SAMPLE_ROWS), n_new),
            jnp.stack(ls_l),
            jnp.stack(sp_l), jnp.stack(ss_l))
```

```python
import functools
import math

import jax
import jax.numpy as jnp
from jax import lax
from jax.experimental import pallas as pl
from jax.experimental.pallas import tpu as pltpu

F32 = jnp.float32
BF16 = jnp.bfloat16

NORM_EPS = 1e-6
GN_EPS = 1e-5
ROPE_BASE = 10000.0
NEG_BIG = -1e30
LOG2E = math.log2(math.e)

V7X_VMEM_BYTES = 64 * 1024 * 1024
V7X_LANES = 128
V7X_BF16_SUBLANES = 16
V7X_MXU_DIM = 256
VMEM_LIMIT = (V7X_VMEM_BYTES * 15) // 16

SAMPLE_ROWS = V7X_BF16_SUBLANES
RET_CHUNK = 128
RET_UNROLL = 8
N_DECAY_PIECES = 3
HEAD_GROUPS = 4
DECODE_PAGES_PER_STEP = 8
FOX_HEADS_PER_STEP = 2


def _tile(n, target, align):
    for t in range(min(n, target), 0, -1):
        if n % t == 0 and t % align == 0:
            return t
    raise ValueError(f"no tile for {n} (target {target}, align {align})")


def _params(*sem):
    return pltpu.CompilerParams(dimension_semantics=sem, vmem_limit_bytes=VMEM_LIMIT)


def _rms_norm(x, g):
    ms = jnp.mean(x * x, axis=-1, keepdims=True)
    return (x * lax.rsqrt(ms + NORM_EPS)) * g


def _nt_dot(a, b):
    return lax.dot_general(a, b, (((1,), (1,)), ((), ())), preferred_element_type=F32)


def _ffn_kernel(x_ref, g_ref, wg_ref, wu_ref, wd_ref, *refs, tf, cast_next):
    if cast_next:
        (ng_ref, nu_ref, nd_ref), refs = refs[:3], refs[3:]
        o_ref, og_ref, ou_ref, od_ref, h_ref = refs
        og_ref[...] = ng_ref[...].astype(BF16)
        ou_ref[...] = nu_ref[...].astype(BF16)
        od_ref[...] = nd_ref[...].astype(BF16)
    else:
        o_ref, h_ref = refs

    @pl.when(pl.program_id(1) == 0)
    def _():
        x = x_ref[...]
        h_ref[...] = _rms_norm(x, g_ref[...]).astype(BF16)
        o_ref[...] = x

    wgu = jnp.concatenate([wg_ref[...].astype(BF16), wu_ref[...].astype(BF16)], axis=1)
    ab = jnp.dot(h_ref[...], wgu, preferred_element_type=F32)
    a = ab[:, :tf]
    b = ab[:, tf:]
    p = (a * jax.nn.sigmoid(a)) * (0.5 * b)
    o_ref[...] += jnp.dot(p.astype(BF16), wd_ref[...].astype(BF16), preferred_element_type=F32)


def _ffn(x, g, weights, layer, next_weights, next_layer, *, tm, tf):
    r, d = x.shape
    w_gate, w_up, w_down = weights
    f = w_down.shape[-2]
    ni, nj = r // tm, f // tf
    if layer is None:
        w_specs = [pl.BlockSpec((d, tf), lambda i, j: (0, j)),
                   pl.BlockSpec((d, tf), lambda i, j: (0, j)),
                   pl.BlockSpec((tf, d), lambda i, j: (j, 0))]
    else:
        w_specs = [pl.BlockSpec((None, d, tf), lambda i, j: (layer, 0, j)),
                   pl.BlockSpec((None, d, tf), lambda i, j: (layer, 0, j)),
                   pl.BlockSpec((None, tf, d), lambda i, j: (layer, j, 0))]
    in_specs = [pl.BlockSpec((tm, d), lambda i, j: (i, 0)), pl.BlockSpec((1, d), lambda i, j: (0, 0))] + w_specs
    args = [x, g, w_gate, w_up, w_down]
    out_specs = [pl.BlockSpec((tm, d), lambda i, j: (i, 0))]
    out_shape = [jax.ShapeDtypeStruct((r, d), F32)]
    cast_next = next_weights is not None
    if cast_next:
        dr = d // ni
        in_specs += [pl.BlockSpec((None, dr, tf), lambda i, j: (next_layer, i, j)),
                     pl.BlockSpec((None, dr, tf), lambda i, j: (next_layer, i, j)),
                     pl.BlockSpec((None, tf, dr), lambda i, j: (next_layer, j, i))]
        args += list(next_weights)
        out_specs += [pl.BlockSpec((dr, tf), lambda i, j: (i, j)),
                      pl.BlockSpec((dr, tf), lambda i, j: (i, j)),
                      pl.BlockSpec((tf, dr), lambda i, j: (j, i))]
        out_shape += [jax.ShapeDtypeStruct((d, f), BF16), jax.ShapeDtypeStruct((d, f), BF16),
                      jax.ShapeDtypeStruct((f, d), BF16)]
    res = pl.pallas_call(
        functools.partial(_ffn_kernel, tf=tf, cast_next=cast_next),
        out_shape=out_shape,
        grid=(ni, nj),
        in_specs=in_specs,
        out_specs=out_specs,
        scratch_shapes=[pltpu.VMEM((tm, d), BF16)],
        compiler_params=_params("parallel", "arbitrary"),
        name="ffn",
    )(*args)
    return res[0], (tuple(res[1:]) if cast_next else None)


def _norm_matmul_kernel(x_ref, g_ref, w_ref, o_ref, h_ref, *, w_transposed):
    @pl.when(pl.program_id(1) == 0)
    def _():
        h_ref[...] = _rms_norm(x_ref[...], g_ref[...]).astype(BF16)

    w = w_ref[...].astype(BF16)
    if w_transposed:
        out = _nt_dot(h_ref[...], w)
    else:
        out = jnp.dot(h_ref[...], w, preferred_element_type=F32)
    o_ref[...] = out.astype(o_ref.dtype)


def _norm_matmul(x, g, w, layer, *, n, tm, tn, out_dtype, w_transposed=False):
    r, d = x.shape
    if w_transposed:
        w_spec = pl.BlockSpec((None, tn, d), lambda i, j: (layer, j, 0))
    else:
        w_spec = pl.BlockSpec((None, d, tn), lambda i, j: (layer, 0, j))
    return pl.pallas_call(
        functools.partial(_norm_matmul_kernel, w_transposed=w_transposed),
        out_shape=jax.ShapeDtypeStruct((r, n), out_dtype),
        grid=(r // tm, n // tn),
        in_specs=[
            pl.BlockSpec((tm, d), lambda i, j: (i, 0)),
            pl.BlockSpec((1, d), lambda i, j: (0, 0)),
            w_spec,
        ],
        out_specs=pl.BlockSpec((tm, tn), lambda i, j: (i, j)),
        scratch_shapes=[pltpu.VMEM((tm, d), BF16)],
        compiler_params=_params("parallel", "arbitrary"),
        name="norm_matmul",
    )(x, g, w)


def _forget_gate_kernel(x_ref, g_ref, w_ref, b_ref, o_ref, *, heads):
    h = _rms_norm(x_ref[...], g_ref[...]).astype(BF16)
    row = lax.broadcasted_iota(jnp.int32, w_ref.shape, 0)
    w = jnp.where(row < heads, w_ref[...], 0.0).astype(BF16)
    z = _nt_dot(h, w) + b_ref[...]
    o_ref[...] = -(jnp.maximum(-z, 0.0) + jnp.log1p(jnp.exp(-jnp.abs(z))))


def _forget_gate(x, g, w_t, layer, b_f, *, row0, heads, tm):
    r, d = x.shape
    n = V7X_LANES
    cb = row0 // n
    return pl.pallas_call(
        functools.partial(_forget_gate_kernel, heads=heads),
        out_shape=jax.ShapeDtypeStruct((r, n), F32),
        grid=(r // tm,),
        in_specs=[
            pl.BlockSpec((tm, d), lambda i: (i, 0)),
            pl.BlockSpec((1, d), lambda i: (0, 0)),
            pl.BlockSpec((None, n, d), lambda i: (layer, cb, 0)),
            pl.BlockSpec((1, n), lambda i: (0, 0)),
        ],
        out_specs=pl.BlockSpec((tm, n), lambda i: (i, 0)),
        compiler_params=_params("parallel"),
        name="forget_gate",
    )(x, g, w_t, b_f)


def _matmul_res_kernel(a_ref, w_ref, r_ref, o_ref):
    w = w_ref[...].astype(BF16)
    o_ref[...] = r_ref[...] + jnp.dot(a_ref[...], w, preferred_element_type=F32)


def _matmul_res(a, w, layer, res, *, tm, tn):
    r, k = a.shape
    n = w.shape[2]
    return pl.pallas_call(
        _matmul_res_kernel,
        out_shape=jax.ShapeDtypeStruct((r, n), F32),
        grid=(r // tm, n // tn),
        in_specs=[
            pl.BlockSpec((tm, k), lambda i, j: (i, 0)),
            pl.BlockSpec((None, k, tn), lambda i, j: (layer, 0, j)),
            pl.BlockSpec((tm, tn), lambda i, j: (i, j)),
        ],
        out_specs=pl.BlockSpec((tm, tn), lambda i, j: (i, j)),
        compiler_params=_params("parallel", "arbitrary"),
        name="matmul_res",
    )(a, w, res)


def _final_norm_kernel(x_ref, g_ref, o_ref):
    o_ref[...] = _rms_norm(x_ref[...], g_ref[...])


def _final_norm(x, g, *, row0, nrows, tr):
    d = x.shape[1]
    b0 = row0 // tr
    return pl.pallas_call(
        _final_norm_kernel,
        out_shape=jax.ShapeDtypeStruct((nrows, d), F32),
        grid=(nrows // tr,),
        in_specs=[pl.BlockSpec((tr, d), lambda i: (b0 + i, 0)), pl.BlockSpec((1, d), lambda i: (0, 0))],
        out_specs=pl.BlockSpec((tr, d), lambda i: (i, 0)),
        compiler_params=_params("parallel"),
        name="final_norm",
    )(x, g)


def _heads_layout_kernel(*refs, heads, dh, aliased):
    k_ref, v_ref = refs[:2]
    ko_ref, vo_ref = refs[2 + 2 * aliased:]
    tr = k_ref.shape[0]
    for src, dst in ((k_ref, ko_ref), (v_ref, vo_ref)):
        for h in range(heads):
            dst[pl.ds(h, tr, stride=heads), :] = src[:, h * dh:(h + 1) * dh]


def _heads_layout(proj, prev, layer, n_layers, *, row0, nrows, tr, heads, dh):
    d = heads * dh
    b0 = row0 // tr
    aliased = prev is not None
    shape = jax.ShapeDtypeStruct((n_layers, nrows * heads, dh), F32)
    in_specs = [pl.BlockSpec((tr, d), lambda i: (b0 + i, 1)),
                pl.BlockSpec((tr, d), lambda i: (b0 + i, 2))]
    args = [proj, proj]
    if aliased:
        in_specs += [pl.BlockSpec(memory_space=pl.ANY)] * 2
        args += list(prev)
    out_spec = pl.BlockSpec((None, tr * heads, dh), lambda i: (layer, i, 0))
    return pl.pallas_call(
        functools.partial(_heads_layout_kernel, heads=heads, dh=dh, aliased=aliased),
        out_shape=[shape, shape],
        grid=(nrows // tr,),
        in_specs=in_specs,
        out_specs=[out_spec, out_spec],
        input_output_aliases={2: 0, 3: 1} if aliased else {},
        compiler_params=_params("parallel"),
        name="heads_layout",
    )(*args)


def _split_bf16(x):
    pieces = []
    rem = x
    for _ in range(N_DECAY_PIECES):
        p = rem.astype(BF16)
        pieces.append(p)
        rem = rem - p.astype(F32)
    return pieces


def _tri_ones(n, lower):
    row = lax.broadcasted_iota(jnp.int32, (n, n), 0)
    col = lax.broadcasted_iota(jnp.int32, (n, n), 1)
    keep = (col <= row) if lower else (row <= col)
    return jnp.where(keep, 1.0, 0.0).astype(BF16)


def _prompt_cumsum_kernel(lf_ref, c_ref, *, blk):
    t = lf_ref.shape[0]
    lower = _tri_ones(blk, lower=True)
    carry = jnp.zeros((1, lf_ref.shape[1]), F32)
    for i in range(t // blk):
        local = carry
        for piece in _split_bf16(lf_ref[i * blk:(i + 1) * blk, :]):
            local = local + jnp.dot(lower, piece, preferred_element_type=F32)
        c_ref[i * blk:(i + 1) * blk, :] = local
        carry = local[blk - 1:blk, :]


def _prompt_cumsum(logf, *, batch, seq):
    w = logf.shape[1]
    return pl.pallas_call(
        functools.partial(_prompt_cumsum_kernel, blk=V7X_LANES),
        out_shape=jax.ShapeDtypeStruct((batch * seq, w), F32),
        grid=(batch,),
        in_specs=[pl.BlockSpec((seq, w), lambda b: (b, 0))],
        out_specs=pl.BlockSpec((seq, w), lambda b: (b, 0)),
        compiler_params=_params("parallel"),
        name="prompt_cumsum",
    )(logf)


def _lane_cumsum_block(xt, upper):
    out = None
    for piece in _split_bf16(xt):
        part = jnp.dot(piece, upper, preferred_element_type=F32)
        out = part if out is None else out + part
    return out


def _regroup(c, spread, valid, heads):
    y = None
    for piece in _split_bf16(c):
        part = jnp.dot(piece, spread, preferred_element_type=F32)
        y = part if y is None else y + part
    out = []
    for i in range(c.shape[0] // heads):
        blk = y[i * heads:(i + 1) * heads]
        rows = [jnp.sum(jnp.where(v, blk, 0.0), axis=0, keepdims=True) for v in valid]
        out.append(jnp.concatenate(rows, axis=0))
    return out


def _decode_cumsum_kernel(pt_ref, *refs, n_pg, heads, page):
    page_refs = refs[:n_pg]
    new_ref = refs[n_pg]
    c_ref, cn_ref, carry_ref = refs[n_pg + 1:]
    g = pl.program_id(1)
    per = heads // HEAD_GROUPS
    nt = new_ref.shape[0]
    nr = n_pg * heads

    @pl.when(g == 0)
    def _():
        carry_ref[...] = jnp.zeros_like(carry_ref)

    def masks(m, head_slot):
        head = lax.broadcasted_iota(jnp.int32, (heads, m), 0)
        col = lax.broadcasted_iota(jnp.int32, (heads, m), 1)
        owns = (head // HEAD_GROUPS) == head_slot(col)
        return [jnp.logical_and(lax.rem(head, HEAD_GROUPS) == grp, owns) for grp in range(HEAD_GROUPS)]

    m_past = page * per
    t_idx = lax.broadcasted_iota(jnp.int32, (page, m_past), 0)
    c_idx = lax.broadcasted_iota(jnp.int32, (page, m_past), 1)
    spread = jnp.where(c_idx // per == t_idx, 1.0, 0.0).astype(BF16)
    upper = _tri_ones(page, lower=False)

    x = jnp.concatenate([page_refs[i][...] for i in range(n_pg)], axis=0)
    within = _lane_cumsum_block(x, upper)
    totals = jnp.broadcast_to(within[:, page - 1:page], (nr, page))
    r_idx = lax.broadcasted_iota(jnp.int32, (nr, nr), 0)
    q_idx = lax.broadcasted_iota(jnp.int32, (nr, nr), 1)
    earlier_page = jnp.logical_and(lax.rem(r_idx, heads) == lax.rem(q_idx, heads), q_idx < r_idx)
    earlier_page = jnp.where(earlier_page, 1.0, 0.0).astype(BF16)
    offset = None
    for piece in _split_bf16(totals):
        part = jnp.dot(earlier_page, piece, preferred_element_type=F32)
        offset = part if offset is None else offset + part
    carry = carry_ref[...]
    local = within + offset + jnp.concatenate([carry] * n_pg, axis=0)
    carry = local[nr - heads:, page - 1:page]
    carry_ref[...] = carry
    grouped = _regroup(local, spread, masks(m_past, lambda col: lax.rem(col, per)), heads)
    for i in range(n_pg):
        c_ref[0, :, i * m_past:(i + 1) * m_past] = grouped[i]

    @pl.when(g == pl.num_programs(1) - 1)
    def _():
        xn = new_ref[...]
        xn = jnp.concatenate([xn, jnp.zeros((V7X_LANES - nt, xn.shape[1]), F32)], axis=0)
        new = _lane_cumsum_block(xn.T[:heads, :], upper) + carry
        t_new = lax.broadcasted_iota(jnp.int32, (V7X_LANES, V7X_LANES), 0)
        c_new = lax.broadcasted_iota(jnp.int32, (V7X_LANES, V7X_LANES), 1)
        hit = jnp.logical_and(lax.rem(c_new, nt) == t_new, c_new < per * nt)
        cn_ref[0] = _regroup(new, jnp.where(hit, 1.0, 0.0).astype(BF16),
                             masks(V7X_LANES, lambda col: col // nt), heads)[0]


def _decode_cumsum(page_table, cache_logf, layer, logf, *, n_prompt_rows, heads):
    bd, n_pages = page_table.shape
    page = cache_logf.shape[3]
    per = heads // HEAD_GROUPS
    assert heads % HEAD_GROUPS == 0 and per * SAMPLE_ROWS <= V7X_LANES and page == V7X_LANES
    n_pg = _tile(n_pages, 16, 1)
    past = n_pages * page

    def page_spec(i):
        return pl.BlockSpec((None, None, heads, page),
                            lambda b, g, pt: (layer, pt[b, g * n_pg + i], 0, 0))

    new_block = n_prompt_rows // SAMPLE_ROWS
    grid_spec = pltpu.PrefetchScalarGridSpec(
        num_scalar_prefetch=1,
        grid=(bd, n_pages // n_pg),
        in_specs=[page_spec(i) for i in range(n_pg)]
        + [pl.BlockSpec((SAMPLE_ROWS, logf.shape[1]), lambda b, g, pt: (new_block + b, 0))],
        out_specs=[
            pl.BlockSpec((1, HEAD_GROUPS, n_pg * page * per), lambda b, g, pt: (b, 0, g)),
            pl.BlockSpec((1, HEAD_GROUPS, V7X_LANES), lambda b, g, pt: (b, 0, 0)),
        ],
        scratch_shapes=[pltpu.VMEM((heads, 1), F32)],
    )
    return pl.pallas_call(
        functools.partial(_decode_cumsum_kernel, n_pg=n_pg, heads=heads, page=page),
        out_shape=[jax.ShapeDtypeStruct((bd, HEAD_GROUPS, past * per), F32),
                   jax.ShapeDtypeStruct((bd, HEAD_GROUPS, V7X_LANES), F32)],
        grid_spec=grid_spec,
        compiler_params=_params("parallel", "arbitrary"),
        name="decode_cumsum",
    )(page_table, *([cache_logf] * n_pg), logf)


def _fox_prompt_kernel(q_ref, k_ref, v_ref, c_ref, o_ref, kaug_ref, vt_ref, acc_ref, *, tq, dh, scale):
    qi = pl.program_id(2)
    hps, nk = vt_ref.shape[0], vt_ref.shape[1]
    lanes = c_ref.shape[1]

    @pl.when(qi == 0)
    def _():
        row = lax.broadcasted_iota(jnp.int32, (lanes, dh), 0)
        col = lax.broadcasted_iota(jnp.int32, (lanes, dh), 1)
        for j in range(nk):
            sl = slice(j * tq, (j + 1) * tq)
            pieces = _split_bf16(c_ref[sl, :] * LOG2E)
            for u in range(hps):
                head = pl.program_id(1) * hps + u
                hc = slice(u * dh, (u + 1) * dh)
                aug = jnp.zeros((tq, dh), F32)
                for i, piece in enumerate(pieces):
                    place = jnp.where(jnp.logical_and(row == head, col == i), 1.0, 0.0).astype(BF16)
                    aug = aug + jnp.dot(piece, place, preferred_element_type=F32)
                kaug_ref[u, j, :, :dh] = k_ref[sl, hc].astype(BF16)
                kaug_ref[u, j, :, dh:] = aug.astype(BF16)
                vt_ref[u, j] = v_ref[sl, hc].T.astype(BF16)

    lane = lax.broadcasted_iota(jnp.int32, (tq, dh), 1)
    minus_ones = jnp.where(lane < N_DECAY_PIECES, -1.0, 0.0).astype(BF16)
    q_aug = [jnp.concatenate([(q_ref[:, u * dh:(u + 1) * dh] * (scale * LOG2E)).astype(BF16), minus_ones],
                             axis=1) for u in range(hps)]

    def absorb(kj, m, l, u, diagonal):
        s = _nt_dot(kaug_ref[u, kj], q_aug[u])
        if diagonal:
            key = lax.broadcasted_iota(jnp.int32, (tq, tq), 0)
            qry = lax.broadcasted_iota(jnp.int32, (tq, tq), 1)
            s = jnp.where(key <= qry, s, NEG_BIG)
        m_new = jnp.maximum(m, jnp.max(s, axis=0, keepdims=True))
        alpha = jnp.exp2(m - m_new)
        p = jnp.exp2(s - m_new)
        l = alpha * l + jnp.sum(p, axis=0, keepdims=True)
        pv = jnp.dot(vt_ref[u, kj], p.astype(BF16), preferred_element_type=F32)
        acc_ref[u] = pv if kj == 0 else alpha * acc_ref[u] + pv
        return m_new, l

    for last in range(nk):
        @pl.when(qi == last)
        def _(last=last):
            for u in range(hps):
                m = jnp.full((1, tq), NEG_BIG, F32)
                l = jnp.zeros((1, tq), F32)
                for kj in range(last + 1):
                    m, l = absorb(kj, m, l, u, kj == last)
                o_ref[:, u * dh:(u + 1) * dh] = (acc_ref[u] / l).T.astype(o_ref.dtype)


def _fox_prompt(proj, c_col, *, rows, batch, seq, heads, dh, tq, hps):
    nq = seq // tq
    hb = heads // hps
    w = hps * dh
    return pl.pallas_call(
        functools.partial(_fox_prompt_kernel, tq=tq, dh=dh, scale=dh ** -0.5),
        out_shape=jax.ShapeDtypeStruct((rows, heads * dh), BF16),
        grid=(batch, hb, nq),
        in_specs=[
            pl.BlockSpec((tq, w), lambda b, h, i: (b * nq + i, h)),
            pl.BlockSpec((seq, w), lambda b, h, i: (b, hb + h)),
            pl.BlockSpec((seq, w), lambda b, h, i: (b, 2 * hb + h)),
            pl.BlockSpec((seq, c_col.shape[1]), lambda b, h, i: (b, 0)),
        ],
        out_specs=pl.BlockSpec((tq, w), lambda b, h, i: (b * nq + i, h)),
        scratch_shapes=[pltpu.VMEM((hps, nq, tq, 2 * dh), BF16),
                        pltpu.VMEM((hps, nq, dh, tq), BF16),
                        pltpu.VMEM((hps, dh, tq), F32)],
        compiler_params=_params("parallel", "parallel", "arbitrary"),
        name="fox_prompt",
    )(proj, proj, proj, c_col)


def _fox_decode_kernel(pt_ref, q_ref, kn_ref, vn_ref, ck_ref, cn_ref, *refs,
                       n_pg, heads, dh, page, n_new, scale):
    k_refs = refs[:n_pg]
    v_refs = refs[n_pg:2 * n_pg]
    o_ref, m_ref, l_ref, acc_ref = refs[2 * n_pg + 1:]
    step = pl.program_id(1)
    nt = q_ref.shape[0]
    per = heads // HEAD_GROUPS
    gr = per * nt
    cols = page * per

    @pl.when(step == 0)
    def _():
        m_ref[...] = jnp.full_like(m_ref, NEG_BIG)
        l_ref[...] = jnp.zeros_like(l_ref)
        acc_ref[...] = jnp.zeros_like(acc_ref)

    def group_rows(ref, grp):
        return ref[pl.ds(grp, cols, stride=HEAD_GROUPS), :].astype(BF16)

    def group_queries(grp):
        parts = [q_ref[:, (grp + HEAD_GROUPS * j) * dh:(grp + HEAD_GROUPS * j + 1) * dh] for j in range(per)]
        return (jnp.concatenate(parts, axis=0) * scale).astype(BF16)

    def update(grp, s, v_blocks, width):
        m_old = m_ref[grp]
        m_new = jnp.maximum(m_old, jnp.max(s, axis=-1, keepdims=True))
        alpha = jnp.exp(m_old - m_new)
        p = jnp.exp(s - m_new)
        l_ref[grp] = alpha * l_ref[grp] + jnp.sum(p, axis=-1, keepdims=True)
        pv = None
        for i, vb in enumerate(v_blocks):
            w = p[:, i * width:(i + 1) * width].astype(BF16)
            part = jnp.dot(w, vb, preferred_element_type=F32)
            pv = part if pv is None else pv + part
        acc_ref[grp] = alpha * acc_ref[grp] + pv
        m_ref[grp] = m_new

    row = lax.broadcasted_iota(jnp.int32, (gr, n_pg * cols), 0)
    col = lax.broadcasted_iota(jnp.int32, (gr, n_pg * cols), 1)
    own = (row // nt) == lax.rem(col, per)
    for grp in range(HEAD_GROUPS):
        qg = group_queries(grp)
        s = jnp.concatenate([_nt_dot(qg, group_rows(k_refs[i], grp)) for i in range(n_pg)], axis=1)
        s = jnp.where(own, s - ck_ref[0, grp:grp + 1, :], NEG_BIG)
        update(grp, s, [group_rows(v_refs[i], grp) for i in range(n_pg)], cols)

    @pl.when(step == pl.num_programs(1) - 1)
    def _():
        row = lax.broadcasted_iota(jnp.int32, (gr, V7X_LANES), 0)
        col = lax.broadcasted_iota(jnp.int32, (gr, V7X_LANES), 1)
        t_col = lax.rem(col, nt)
        visible = jnp.logical_and(jnp.logical_and((row // nt) == (col // nt), t_col <= lax.rem(row, nt)),
                                  jnp.logical_and(t_col < n_new, col < gr))
        pad = jnp.zeros((V7X_LANES - gr, dh), F32)
        for grp in range(HEAD_GROUPS):
            head_cols = [slice((grp + HEAD_GROUPS * j) * dh, (grp + HEAD_GROUPS * j + 1) * dh) for j in range(per)]
            k_new = jnp.concatenate([kn_ref[:, c] for c in head_cols] + [pad], axis=0).astype(BF16)
            v_new = jnp.concatenate([vn_ref[:, c] for c in head_cols] + [pad], axis=0).astype(BF16)
            s = _nt_dot(group_queries(grp), k_new)
            s = jnp.where(visible, s - cn_ref[0, grp:grp + 1, :], NEG_BIG)
            update(grp, s, [v_new], V7X_LANES)
            out = acc_ref[grp] / l_ref[grp]
            for j, c in enumerate(head_cols):
                o_ref[:, c] = out[j * nt:(j + 1) * nt].astype(o_ref.dtype)


def _fox_decode(page_table, proj, ck_past, c_new, cache_k, cache_v, layer, attn, *,
                n_prompt_rows, heads, dh, n_new, n_pg):
    bd, n_pages = page_table.shape
    page = cache_k.shape[2] // heads
    per = heads // HEAD_GROUPS
    gr = per * SAMPLE_ROWS
    hd = heads * dh
    sblk = n_prompt_rows // SAMPLE_ROWS

    def row_spec(col):
        return pl.BlockSpec((SAMPLE_ROWS, hd), lambda b, g, pt: (sblk + b, col))

    def page_spec(i):
        return pl.BlockSpec((None, None, page * heads, dh),
                            lambda b, g, pt: (layer, pt[b, g * n_pg + i], 0, 0))

    n_in = 5 + 2 * n_pg
    grid_spec = pltpu.PrefetchScalarGridSpec(
        num_scalar_prefetch=1,
        grid=(bd, n_pages // n_pg),
        in_specs=[row_spec(0), row_spec(1), row_spec(2),
                  pl.BlockSpec((1, HEAD_GROUPS, n_pg * page * per), lambda b, g, pt: (b, 0, g)),
                  pl.BlockSpec((1, HEAD_GROUPS, V7X_LANES), lambda b, g, pt: (b, 0, 0))]
        + [page_spec(i) for i in range(n_pg)] * 2
        + [pl.BlockSpec(memory_space=pl.ANY)],
        out_specs=pl.BlockSpec((SAMPLE_ROWS, hd), lambda b, g, pt: (sblk + b, 0)),
        scratch_shapes=[pltpu.VMEM((HEAD_GROUPS, gr, 1), F32),
                        pltpu.VMEM((HEAD_GROUPS, gr, 1), F32),
                        pltpu.VMEM((HEAD_GROUPS, gr, dh), F32)],
    )
    return pl.pallas_call(
        functools.partial(_fox_decode_kernel, n_pg=n_pg, heads=heads, dh=dh, page=page,
                          n_new=n_new, scale=dh ** -0.5),
        out_shape=jax.ShapeDtypeStruct(attn.shape, attn.dtype),
        grid_spec=grid_spec,
        input_output_aliases={n_in + 1: 0},
        compiler_params=_params("parallel", "arbitrary"),
        name="fox_decode",
    )(page_table, proj, proj, proj, ck_past, c_new,
      *([cache_k] * n_pg), *([cache_v] * n_pg), attn)


def _ret_tables(heads, chunk, t_real):
    lg = jnp.log1p(-jnp.exp2(-5.0 - jnp.arange(heads, dtype=F32)))
    n = jnp.arange(chunk, dtype=F32)
    diff = n[:, None] - n[None, :]
    causal = diff >= 0
    dmat = jnp.where(causal[None], jnp.exp(jnp.where(causal, diff, 0.0)[None] * lg[:, None, None]), 0.0)
    cross = jnp.exp((n[None, :] + 1.0) * lg[:, None])
    kdec = jnp.where(n[None, :] < t_real, jnp.exp((t_real - 1.0 - n)[None, :] * lg[:, None]), 0.0)
    full = jnp.broadcast_to(jnp.exp(t_real * lg)[:, None], (heads, chunk))
    dvec = jnp.stack([cross, kdec, full], axis=-1)
    return dmat.astype(F32), dvec.astype(F32)


def _rope_tables(positions, dk):
    half = dk // 2
    inv = ROPE_BASE ** (-jnp.arange(half, dtype=F32) / half)
    ang = positions.astype(F32)[:, None] * inv[None, :]
    return jnp.cos(ang), jnp.sin(ang)


def _ret_kernel(*refs, has_state, chunk, rows, n_chunks, dk, dv, hps):
    if has_state:
        q_ref, k_ref, v_ref, g_ref, cos_ref, sin_ref, dmat_ref, dvec_ref, s0_ref, y_ref, st_ref = refs
        st_ref[...] = s0_ref[...]
    else:
        q_ref, k_ref, v_ref, g_ref, cos_ref, sin_ref, dmat_ref, dvec_ref, y_ref, st_ref = refs
        st_ref[...] = jnp.zeros_like(st_ref)
    half = dk // 2

    def padded(x):
        x = x.astype(F32)
        if rows == chunk:
            return x
        return jnp.concatenate([x, jnp.zeros((chunk - rows, x.shape[1]), F32)], axis=0)

    tables = []
    for u in range(hps):
        dvec = dvec_ref[u]
        tables.append((dmat_ref[u], dvec[:, 0:1], dvec[:, 1:2], dvec[0:1, 2:3]))

    def head_step(c, u):
        sl = pl.ds(pl.multiple_of(c * rows, rows), rows)
        qk_cols = slice(u * dk, (u + 1) * dk)
        v_cols = slice(u * dv, (u + 1) * dv)
        dmat, cross_d, k_d, full_d = tables[u]
        cos = padded(cos_ref[sl, :])
        sin = padded(sin_ref[sl, :])

        def rope(x):
            x1 = x[:, :half]
            x2 = x[:, half:]
            return jnp.concatenate([x1 * cos - x2 * sin, x1 * sin + x2 * cos], axis=1)

        q = rope(padded(q_ref[sl, qk_cols]))
        k = rope(padded(k_ref[sl, qk_cols])) * (dk ** -0.5)
        v = padded(v_ref[sl, v_cols]).astype(BF16)
        qb = q.astype(BF16)
        s = _nt_dot(qb, k.astype(BF16)) * dmat
        state = st_ref[0, u]
        y = jnp.dot(s.astype(BF16), v, preferred_element_type=F32)
        y = y + jnp.dot(qb, state.astype(BF16), preferred_element_type=F32) * cross_d
        kdec_t = (k * k_d).T.astype(BF16)
        st_ref[0, u] = full_d * state + jnp.dot(kdec_t, v, preferred_element_type=F32)

        mu = jnp.mean(y, axis=-1, keepdims=True)
        yc = y - mu
        var = jnp.mean(yc * yc, axis=-1, keepdims=True)
        yn = yc * lax.rsqrt(var + GN_EPS)
        gate = padded(g_ref[sl, v_cols])
        out = yn * (gate * jax.nn.sigmoid(gate))
        y_ref[sl, v_cols] = out[:rows].astype(y_ref.dtype)

    def step(c, carry):
        for u in range(hps):
            head_step(c, u)
        return carry

    lax.fori_loop(0, n_chunks, step, 0, unroll=math.gcd(n_chunks, RET_UNROLL))


def _retention(proj, cos, sin, state0, layer, n_layers, y_prev, st_prev, *, row0, batch, n_chunks,
               rows, t_real, heads, dk, dv, hps):
    chunk = max(rows, RET_CHUNK)
    dmat, dvec = _ret_tables(heads, chunk, t_real)
    seq_rows = n_chunks * rows
    r0 = row0 // seq_rows
    qk_w = heads * dk
    has_state = state0 is not None

    def tok_spec(width, col0):
        cb = col0 // (hps * width)
        return pl.BlockSpec((seq_rows, hps * width), lambda b, h: (r0 + b, cb + h))

    in_specs = [
        tok_spec(dk, 0), tok_spec(dk, qk_w), tok_spec(dv, 2 * qk_w), tok_spec(dv, 2 * qk_w + heads * dv),
        pl.BlockSpec((seq_rows, dk // 2), lambda b, h: (0, 0)),
        pl.BlockSpec((seq_rows, dk // 2), lambda b, h: (0, 0)),
        pl.BlockSpec((hps, chunk, chunk), lambda b, h: (h, 0, 0)),
        pl.BlockSpec((hps, chunk, 3), lambda b, h: (h, 0, 0)),
    ]
    args = [proj, proj, proj, proj, cos, sin, dmat, dvec]
    if has_state:
        in_specs.append(pl.BlockSpec((None, 1, hps, dk, dv), lambda b, h: (layer, b, h, 0, 0)))
        args.append(state0)
    n_used = len(args)
    aliases = {}
    for prev, out_idx in ((y_prev, 0), (st_prev, 1)):
        if prev is not None:
            aliases[len(args)] = out_idx
            in_specs.append(pl.BlockSpec(memory_space=pl.ANY))
            args.append(prev)
    n_in = len(args)

    def kernel(*refs):
        refs = refs[:n_used] + refs[n_in:]
        _ret_kernel(*refs, has_state=has_state, chunk=chunk, rows=rows, n_chunks=n_chunks,
                    dk=dk, dv=dv, hps=hps)

    return pl.pallas_call(
        kernel,
        out_shape=[jax.ShapeDtypeStruct((proj.shape[0], heads * dv), BF16),
                   jax.ShapeDtypeStruct((n_layers, batch, heads, dk, dv), F32)],
        grid=(batch, heads // hps),
        in_specs=in_specs,
        out_specs=[
            pl.BlockSpec((seq_rows, hps * dv), lambda b, h: (r0 + b, h)),
            pl.BlockSpec((None, 1, hps, dk, dv), lambda b, h: (layer, b, h, 0, 0)),
        ],
        input_output_aliases=aliases,
        compiler_params=_params("parallel", "parallel"),
        name="retention",
    )(*args)


def kernel(x_prompt, x_sample, cache_fox_k, cache_fox_v, cache_fox_logf, state_ret, page_table, norm_ffn1, ffn1_w_gate, ffn1_w_up, ffn1_w_down, norm_mix, fox_w_in, fox_b_f, fox_w_out, ret_w_in, ret_w_out, norm_ffn2, ffn2_w_gate, ffn2_w_up, ffn2_w_down, norm_final):
    batch, seq, d = x_prompt.shape
    bd, n_new, _ = x_sample.shape
    depth = norm_ffn1.shape[0]
    f = ffn1_w_gate.shape[2]
    n_fox, pool, page, fox_heads, dh = cache_fox_k.shape
    ret_heads, dk, dv = state_ret.shape[2:]
    n_pages = page_table.shape[1]
    past = n_pages * page
    assert n_new <= SAMPLE_ROWS and fox_heads * dh == d and ret_heads * dk == d

    n_p = batch * seq
    n_s = bd * SAMPLE_ROWS
    rows = n_p + n_s
    tm = _tile(rows, 1100, V7X_BF16_SUBLANES)
    tf = _tile(f, V7X_MXU_DIM, V7X_LANES)
    tf_wide = _tile(f, 2 * V7X_MXU_DIM, V7X_LANES)
    tn = _tile(d, 1024, V7X_LANES)
    tn_deep = _tile(d, 512, V7X_LANES)
    tq = _tile(seq, 1024, V7X_LANES)
    tr = _tile(n_p, 512, n_s)
    n_pg = _tile(n_pages, DECODE_PAGES_PER_STEP, 1)

    xs = jnp.pad(x_sample, ((0, 0), (0, SAMPLE_ROWS - n_new), (0, 0)))
    x = jnp.concatenate([x_prompt.reshape(n_p, d), xs.reshape(n_s, d)], axis=0)

    cache_k2 = cache_fox_k.reshape(n_fox, pool, page * fox_heads, dh)
    cache_v2 = cache_fox_v.reshape(n_fox, pool, page * fox_heads, dh)
    cache_logf_t = jnp.swapaxes(cache_fox_logf, 2, 3)
    fox_bias = jnp.pad(fox_b_f, ((0, 0), (0, V7X_LANES - fox_heads)))
    fox_w_in_t = jnp.swapaxes(fox_w_in, 1, 2)

    cos_p, sin_p = _rope_tables(jnp.arange(seq), dk)
    cos_s, sin_s = _rope_tables(past + jnp.arange(SAMPLE_ROWS), dk)
    n_ret = depth // 2

    def split_rows(a):
        w = a.shape[1]
        return (a[:n_p].reshape(batch, seq, w),
                a[n_p:].reshape(bd, SAMPLE_ROWS, w)[:, :n_new])

    ffn_stacks = ((ffn1_w_gate, ffn1_w_up, ffn1_w_down), (ffn2_w_gate, ffn2_w_up, ffn2_w_down))
    ffn_norms = (norm_ffn1, norm_ffn2)
    ffn_ready = [None]

    def run_ffn(x, which, i):
        last = which == 1 and i == depth - 1
        nxt = None if last else ffn_stacks[1 - which]
        nxt_layer = i + which
        if ffn_ready[0] is None:
            weights, layer, tf_k = ffn_stacks[which], i, tf
        else:
            weights, layer, tf_k = ffn_ready[0], None, tf_wide
        x, ffn_ready[0] = _ffn(x, ffn_norms[which][i][None], weights, layer, nxt, nxt_layer, tm=tm, tf=tf_k)
        return x

    kv_p = kv_s = st_p = st_s = None
    lp_l, ls_l = [], []
    for i in range(depth):
        x = run_ffn(x, 0, i)
        j = i // 2
        g_mix = norm_mix[i][None]
        if i % 2 == 0:
            proj = _norm_matmul(x, g_mix, fox_w_in_t, j, n=3 * d, tm=tm, tn=tn, out_dtype=F32,
                                w_transposed=True)
            logf = _forget_gate(x, g_mix, fox_w_in_t, j, fox_bias[j][None], row0=3 * d, heads=fox_heads, tm=tm)
            c_col = _prompt_cumsum(logf, batch=batch, seq=seq)
            attn = _fox_prompt(proj, c_col, rows=rows, batch=batch, seq=seq, heads=fox_heads, dh=dh, tq=tq,
                               hps=FOX_HEADS_PER_STEP)
            c_past, c_new = _decode_cumsum(page_table, cache_logf_t, j, logf,
                                           n_prompt_rows=n_p, heads=fox_heads)
            attn = _fox_decode(page_table, proj, c_past, c_new, cache_k2, cache_v2, j, attn,
                               n_prompt_rows=n_p, heads=fox_heads, dh=dh, n_new=n_new, n_pg=n_pg)
            x = _matmul_res(attn, fox_w_out, j, x, tm=tm, tn=tn)
            kv_p = _heads_layout(proj, kv_p, j, n_fox, row0=0, nrows=n_p, tr=tr, heads=fox_heads, dh=dh)
            kv_s = _heads_layout(proj, kv_s, j, n_fox, row0=n_p, nrows=n_s, tr=n_s, heads=fox_heads, dh=dh)
            lp, ls = split_rows(logf[:, :fox_heads])
            lp_l.append(lp)
            ls_l.append(ls)
        else:
            proj = _norm_matmul(x, g_mix, ret_w_in, j, n=ret_w_in.shape[2], tm=tm, tn=tn, out_dtype=BF16)
            y, st_p = _retention(proj, cos_p, sin_p, None, j, n_ret, None, st_p, row0=0, batch=batch,
                                 n_chunks=seq // RET_CHUNK, rows=RET_CHUNK, t_real=RET_CHUNK,
                                 heads=ret_heads, dk=dk, dv=dv, hps=1)
            y, st_s = _retention(proj, cos_s, sin_s, state_ret, j, n_ret, y, st_s, row0=n_p, batch=bd,
                                 n_chunks=1, rows=SAMPLE_ROWS, t_real=n_new,
                                 heads=ret_heads, dk=dk, dv=dv, hps=ret_heads)
            x = _matmul_res(y, ret_w_out, j, x, tm=tm, tn=tn_deep)
        x = run_ffn(x, 1, i)

    g_fin = norm_final[None]
    y_prompt = _final_norm(x, g_fin, row0=0, nrows=n_p, tr=tr).reshape(batch, seq, d)
    y_sample = _final_norm(x, g_fin, row0=n_p, nrows=n_s, tr=n_s).reshape(bd, SAMPLE_ROWS, d)[:, :n_new]

    def heads_out(a, lead, n_tok):
        return a.reshape((n_fox,) + lead + (fox_heads, dh))[:, :, :n_tok]

    return (y_prompt, y_sample,
            heads_out(kv_p[0], (batch, seq), seq), heads_out(kv_p[1], (batch, seq), seq),
            jnp.stack(lp_l),
            heads_out(kv_s[0], (bd, SAMPLE_ROWS), n_new), heads_out(kv_s[1], (bd, SAMPLE_ROWS), n_new),
            jnp.stack(ls_l),
            st_p, st_s)
```

```python
import functools
import math

import jax
import jax.numpy as jnp
from jax import lax
from jax.experimental import pallas as pl
from jax.experimental.pallas import tpu as pltpu

F32 = jnp.float32
BF16 = jnp.bfloat16

NORM_EPS = 1e-6
GN_EPS = 1e-5
ROPE_BASE = 10000.0
NEG_BIG = -1e30
LOG2E = math.log2(math.e)

V7X_VMEM_BYTES = 64 * 1024 * 1024
V7X_LANES = 128
V7X_BF16_SUBLANES = 16
V7X_MXU_DIM = 256
VMEM_LIMIT = (V7X_VMEM_BYTES * 15) // 16

SAMPLE_ROWS = V7X_BF16_SUBLANES
RET_CHUNK = 128
RET_UNROLL = 8
N_DECAY_PIECES = 3
HEAD_GROUPS = 4
DECODE_PAGES_PER_STEP = 8
RET_HEADS_PER_STEP = 2
FOX_HEADS_PER_STEP = 4


def _tile(n, target, align):
    for t in range(min(n, target), 0, -1):
        if n % t == 0 and t % align == 0:
            return t
    raise ValueError(f"no tile for {n} (target {target}, align {align})")


def _params(*sem):
    return pltpu.CompilerParams(dimension_semantics=sem, vmem_limit_bytes=VMEM_LIMIT)


def _rms_norm(x, g):
    ms = jnp.mean(x * x, axis=-1, keepdims=True)
    return (x * lax.rsqrt(ms + NORM_EPS)) * g


def _nt_dot(a, b):
    return lax.dot_general(a, b, (((1,), (1,)), ((), ())), preferred_element_type=F32)


def _ffn_kernel(x_ref, g_ref, wg_ref, wu_ref, wd_ref, *refs, tf, cast_next):
    if cast_next:
        (ng_ref, nu_ref, nd_ref), refs = refs[:3], refs[3:]
        o_ref, og_ref, ou_ref, od_ref, h_ref = refs
        og_ref[...] = ng_ref[...].astype(BF16)
        ou_ref[...] = nu_ref[...].astype(BF16)
        od_ref[...] = nd_ref[...].astype(BF16)
    else:
        o_ref, h_ref = refs

    @pl.when(pl.program_id(1) == 0)
    def _():
        x = x_ref[...]
        h_ref[...] = _rms_norm(x, g_ref[...]).astype(BF16)
        o_ref[...] = x

    wgu = jnp.concatenate([wg_ref[...].astype(BF16), wu_ref[...].astype(BF16)], axis=1)
    ab = jnp.dot(h_ref[...], wgu, preferred_element_type=F32)
    a = ab[:, :tf]
    b = ab[:, tf:]
    p = (a * jax.nn.sigmoid(a)) * (0.5 * b)
    o_ref[...] += jnp.dot(p.astype(BF16), wd_ref[...].astype(BF16), preferred_element_type=F32)


def _ffn(x, g, weights, layer, next_weights, next_layer, *, tm, tf):
    r, d = x.shape
    w_gate, w_up, w_down = weights
    f = w_down.shape[-2]
    ni, nj = r // tm, f // tf
    if layer is None:
        w_specs = [pl.BlockSpec((d, tf), lambda i, j: (0, j)),
                   pl.BlockSpec((d, tf), lambda i, j: (0, j)),
                   pl.BlockSpec((tf, d), lambda i, j: (j, 0))]
    else:
        w_specs = [pl.BlockSpec((None, d, tf), lambda i, j: (layer, 0, j)),
                   pl.BlockSpec((None, d, tf), lambda i, j: (layer, 0, j)),
                   pl.BlockSpec((None, tf, d), lambda i, j: (layer, j, 0))]
    in_specs = [pl.BlockSpec((tm, d), lambda i, j: (i, 0)), pl.BlockSpec((1, d), lambda i, j: (0, 0))] + w_specs
    args = [x, g, w_gate, w_up, w_down]
    out_specs = [pl.BlockSpec((tm, d), lambda i, j: (i, 0))]
    out_shape = [jax.ShapeDtypeStruct((r, d), F32)]
    cast_next = next_weights is not None
    if cast_next:
        assert d % ni == 0 and (d // ni) % V7X_LANES == 0
        dr = d // ni
        in_specs += [pl.BlockSpec((None, dr, tf), lambda i, j: (next_layer, i, j)),
                     pl.BlockSpec((None, dr, tf), lambda i, j: (next_layer, i, j)),
                     pl.BlockSpec((None, tf, dr), lambda i, j: (next_layer, j, i))]
        args += list(next_weights)
        out_specs += [pl.BlockSpec((dr, tf), lambda i, j: (i, j)),
                      pl.BlockSpec((dr, tf), lambda i, j: (i, j)),
                      pl.BlockSpec((tf, dr), lambda i, j: (j, i))]
        out_shape += [jax.ShapeDtypeStruct((d, f), BF16), jax.ShapeDtypeStruct((d, f), BF16),
                      jax.ShapeDtypeStruct((f, d), BF16)]
    res = pl.pallas_call(
        functools.partial(_ffn_kernel, tf=tf, cast_next=cast_next),
        out_shape=out_shape,
        grid=(ni, nj),
        in_specs=in_specs,
        out_specs=out_specs,
        scratch_shapes=[pltpu.VMEM((tm, d), BF16)],
        compiler_params=_params("parallel", "arbitrary"),
        name="ffn",
    )(*args)
    return res[0], (tuple(res[1:]) if cast_next else None)


def _norm_matmul_kernel(x_ref, g_ref, w_ref, o_ref, h_ref, *, w_transposed):
    @pl.when(pl.program_id(1) == 0)
    def _():
        h_ref[...] = _rms_norm(x_ref[...], g_ref[...]).astype(BF16)

    w = w_ref[...].astype(BF16)
    if w_transposed:
        out = _nt_dot(h_ref[...], w)
    else:
        out = jnp.dot(h_ref[...], w, preferred_element_type=F32)
    o_ref[...] = out.astype(o_ref.dtype)


def _norm_matmul(x, g, w, layer, *, n, tm, tn, out_dtype, w_transposed=False):
    r, d = x.shape
    if w_transposed:
        w_spec = pl.BlockSpec((None, tn, d), lambda i, j: (layer, j, 0))
    else:
        w_spec = pl.BlockSpec((None, d, tn), lambda i, j: (layer, 0, j))
    return pl.pallas_call(
        functools.partial(_norm_matmul_kernel, w_transposed=w_transposed),
        out_shape=jax.ShapeDtypeStruct((r, n), out_dtype),
        grid=(r // tm, n // tn),
        in_specs=[
            pl.BlockSpec((tm, d), lambda i, j: (i, 0)),
            pl.BlockSpec((1, d), lambda i, j: (0, 0)),
            w_spec,
        ],
        out_specs=pl.BlockSpec((tm, tn), lambda i, j: (i, j)),
        scratch_shapes=[pltpu.VMEM((tm, d), BF16)],
        compiler_params=_params("parallel", "arbitrary"),
        name="norm_matmul",
    )(x, g, w)


def _forget_gate_kernel(x_ref, g_ref, w_ref, b_ref, o_ref, *, heads):
    h = _rms_norm(x_ref[...], g_ref[...]).astype(BF16)
    row = lax.broadcasted_iota(jnp.int32, w_ref.shape, 0)
    w = jnp.where(row < heads, w_ref[...], 0.0).astype(BF16)
    z = _nt_dot(h, w) + b_ref[...]
    o_ref[...] = -(jnp.maximum(-z, 0.0) + jnp.log1p(jnp.exp(-jnp.abs(z))))


def _forget_gate(x, g, w_t, layer, b_f, *, row0, heads, tm):
    r, d = x.shape
    n = V7X_LANES
    cb = row0 // n
    return pl.pallas_call(
        functools.partial(_forget_gate_kernel, heads=heads),
        out_shape=jax.ShapeDtypeStruct((r, n), F32),
        grid=(r // tm,),
        in_specs=[
            pl.BlockSpec((tm, d), lambda i: (i, 0)),
            pl.BlockSpec((1, d), lambda i: (0, 0)),
            pl.BlockSpec((None, n, d), lambda i: (layer, cb, 0)),
            pl.BlockSpec((1, n), lambda i: (0, 0)),
        ],
        out_specs=pl.BlockSpec((tm, n), lambda i: (i, 0)),
        compiler_params=_params("parallel"),
        name="forget_gate",
    )(x, g, w_t, b_f)


def _matmul_res_kernel(a_ref, w_ref, r_ref, o_ref):
    w = w_ref[...].astype(BF16)
    o_ref[...] = r_ref[...] + jnp.dot(a_ref[...], w, preferred_element_type=F32)


def _matmul_res(a, w, layer, res, *, tm, tn):
    r, k = a.shape
    n = w.shape[2]
    return pl.pallas_call(
        _matmul_res_kernel,
        out_shape=jax.ShapeDtypeStruct((r, n), F32),
        grid=(r // tm, n // tn),
        in_specs=[
            pl.BlockSpec((tm, k), lambda i, j: (i, 0)),
            pl.BlockSpec((None, k, tn), lambda i, j: (layer, 0, j)),
            pl.BlockSpec((tm, tn), lambda i, j: (i, j)),
        ],
        out_specs=pl.BlockSpec((tm, tn), lambda i, j: (i, j)),
        compiler_params=_params("parallel", "arbitrary"),
        name="matmul_res",
    )(a, w, res)


def _final_norm_kernel(x_ref, g_ref, o_ref):
    o_ref[...] = _rms_norm(x_ref[...], g_ref[...])


def _final_norm(x, g, *, row0, nrows, tr):
    d = x.shape[1]
    b0 = row0 // tr
    return pl.pallas_call(
        _final_norm_kernel,
        out_shape=jax.ShapeDtypeStruct((nrows, d), F32),
        grid=(nrows // tr,),
        in_specs=[pl.BlockSpec((tr, d), lambda i: (b0 + i, 0)), pl.BlockSpec((1, d), lambda i: (0, 0))],
        out_specs=pl.BlockSpec((tr, d), lambda i: (i, 0)),
        compiler_params=_params("parallel"),
        name="final_norm",
    )(x, g)


def _heads_layout_kernel(*refs, heads, dh, aliased):
    k_ref, v_ref = refs[:2]
    ko_ref, vo_ref = refs[2 + 2 * aliased:]
    tr = k_ref.shape[0]
    for src, dst in ((k_ref, ko_ref), (v_ref, vo_ref)):
        for h in range(heads):
            dst[pl.ds(h, tr, stride=heads), :] = src[:, h * dh:(h + 1) * dh]


def _heads_layout(proj, prev, layer, n_layers, *, row0, nrows, tr, heads, dh):
    d = heads * dh
    b0 = row0 // tr
    aliased = prev is not None
    shape = jax.ShapeDtypeStruct((n_layers, nrows * heads, dh), F32)
    in_specs = [pl.BlockSpec((tr, d), lambda i: (b0 + i, 1)),
                pl.BlockSpec((tr, d), lambda i: (b0 + i, 2))]
    args = [proj, proj]
    if aliased:
        in_specs += [pl.BlockSpec(memory_space=pl.ANY)] * 2
        args += list(prev)
    out_spec = pl.BlockSpec((None, tr * heads, dh), lambda i: (layer, i, 0))
    return pl.pallas_call(
        functools.partial(_heads_layout_kernel, heads=heads, dh=dh, aliased=aliased),
        out_shape=[shape, shape],
        grid=(nrows // tr,),
        in_specs=in_specs,
        out_specs=[out_spec, out_spec],
        input_output_aliases={2: 0, 3: 1} if aliased else {},
        compiler_params=_params("parallel"),
        name="heads_layout",
    )(*args)


def _split_bf16(x):
    pieces = []
    rem = x
    for _ in range(N_DECAY_PIECES):
        p = rem.astype(BF16)
        pieces.append(p)
        rem = rem - p.astype(F32)
    return pieces


def _tri_ones(n, lower):
    row = lax.broadcasted_iota(jnp.int32, (n, n), 0)
    col = lax.broadcasted_iota(jnp.int32, (n, n), 1)
    keep = (col <= row) if lower else (row <= col)
    return jnp.where(keep, 1.0, 0.0).astype(BF16)


def _prompt_cumsum_kernel(lf_ref, c_ref, *, blk):
    t = lf_ref.shape[0]
    lower = _tri_ones(blk, lower=True)
    carry = jnp.zeros((1, lf_ref.shape[1]), F32)
    for i in range(t // blk):
        local = carry
        for piece in _split_bf16(lf_ref[i * blk:(i + 1) * blk, :]):
            local = local + jnp.dot(lower, piece, preferred_element_type=F32)
        c_ref[i * blk:(i + 1) * blk, :] = local
        carry = local[blk - 1:blk, :]


def _prompt_cumsum(logf, *, batch, seq):
    w = logf.shape[1]
    return pl.pallas_call(
        functools.partial(_prompt_cumsum_kernel, blk=V7X_LANES),
        out_shape=jax.ShapeDtypeStruct((batch * seq, w), F32),
        grid=(batch,),
        in_specs=[pl.BlockSpec((seq, w), lambda b: (b, 0))],
        out_specs=pl.BlockSpec((seq, w), lambda b: (b, 0)),
        compiler_params=_params("parallel"),
        name="prompt_cumsum",
    )(logf)


def _lane_cumsum_block(xt, upper):
    out = None
    for piece in _split_bf16(xt):
        part = jnp.dot(piece, upper, preferred_element_type=F32)
        out = part if out is None else out + part
    return out


def _regroup(c, spread, valid, heads):
    y = None
    for piece in _split_bf16(c):
        part = jnp.dot(piece, spread, preferred_element_type=F32)
        y = part if y is None else y + part
    out = []
    for i in range(c.shape[0] // heads):
        blk = y[i * heads:(i + 1) * heads]
        rows = [jnp.sum(jnp.where(v, blk, 0.0), axis=0, keepdims=True) for v in valid]
        out.append(jnp.concatenate(rows, axis=0))
    return out


def _group_masks(heads, m, head_slot):
    head = lax.broadcasted_iota(jnp.int32, (heads, m), 0)
    col = lax.broadcasted_iota(jnp.int32, (heads, m), 1)
    owns = (head // HEAD_GROUPS) == head_slot(col)
    return [jnp.logical_and(lax.rem(head, HEAD_GROUPS) == grp, owns) for grp in range(HEAD_GROUPS)]


def _past_decay(page_refs, carry, heads, page):
    n_pg = len(page_refs)
    per = heads // HEAD_GROUPS
    nr = n_pg * heads
    m_past = page * per
    t_idx = lax.broadcasted_iota(jnp.int32, (page, m_past), 0)
    c_idx = lax.broadcasted_iota(jnp.int32, (page, m_past), 1)
    spread = jnp.where(c_idx // per == t_idx, 1.0, 0.0).astype(BF16)
    upper = _tri_ones(page, lower=False)

    x = jnp.concatenate([ref[...] for ref in page_refs], axis=0)
    within = _lane_cumsum_block(x, upper)
    totals = jnp.broadcast_to(within[:, page - 1:page], (nr, page))
    r_idx = lax.broadcasted_iota(jnp.int32, (nr, nr), 0)
    q_idx = lax.broadcasted_iota(jnp.int32, (nr, nr), 1)
    earlier_page = jnp.logical_and(lax.rem(r_idx, heads) == lax.rem(q_idx, heads), q_idx < r_idx)
    earlier_page = jnp.where(earlier_page, 1.0, 0.0).astype(BF16)
    offset = None
    for piece in _split_bf16(totals):
        part = jnp.dot(earlier_page, piece, preferred_element_type=F32)
        offset = part if offset is None else offset + part
    local = within + offset + jnp.concatenate([carry] * n_pg, axis=0)
    grouped = _regroup(local, spread, _group_masks(heads, m_past, lambda col: lax.rem(col, per)), heads)
    return grouped, local[nr - heads:, page - 1:page]


def _new_token_decay(new_ref, carry, heads):
    nt = new_ref.shape[0]
    per = heads // HEAD_GROUPS
    xn = jnp.concatenate([new_ref[...], jnp.zeros((V7X_LANES - nt, new_ref.shape[1]), F32)], axis=0)
    new = _lane_cumsum_block(xn.T[:heads, :], _tri_ones(V7X_LANES, lower=False)) + carry
    t_new = lax.broadcasted_iota(jnp.int32, (V7X_LANES, V7X_LANES), 0)
    c_new = lax.broadcasted_iota(jnp.int32, (V7X_LANES, V7X_LANES), 1)
    hit = jnp.logical_and(lax.rem(c_new, nt) == t_new, c_new < per * nt)
    return _regroup(new, jnp.where(hit, 1.0, 0.0).astype(BF16),
                    _group_masks(heads, V7X_LANES, lambda col: col // nt), heads)[0]


def _fox_prompt_kernel(q_ref, k_ref, v_ref, c_ref, o_ref, kaug_ref, vt_ref, acc_ref, *, tq, dh, scale):
    qi = pl.program_id(2)
    hps, nk = vt_ref.shape[0], vt_ref.shape[1]
    lanes = c_ref.shape[1]

    @pl.when(qi == 0)
    def _():
        row = lax.broadcasted_iota(jnp.int32, (lanes, dh), 0)
        col = lax.broadcasted_iota(jnp.int32, (lanes, dh), 1)
        for j in range(nk):
            sl = slice(j * tq, (j + 1) * tq)
            pieces = _split_bf16(c_ref[sl, :] * LOG2E)
            for u in range(hps):
                head = pl.program_id(1) * hps + u
                hc = slice(u * dh, (u + 1) * dh)
                aug = jnp.zeros((tq, dh), F32)
                for i, piece in enumerate(pieces):
                    place = jnp.where(jnp.logical_and(row == head, col == i), 1.0, 0.0).astype(BF16)
                    aug = aug + jnp.dot(piece, place, preferred_element_type=F32)
                kaug_ref[u, j, :, :dh] = k_ref[sl, hc].astype(BF16)
                kaug_ref[u, j, :, dh:] = aug.astype(BF16)
                vt_ref[u, j] = v_ref[sl, hc].T.astype(BF16)

    lane = lax.broadcasted_iota(jnp.int32, (tq, dh), 1)
    minus_ones = jnp.where(lane < N_DECAY_PIECES, -1.0, 0.0).astype(BF16)
    q_aug = [jnp.concatenate([(q_ref[:, u * dh:(u + 1) * dh] * (scale * LOG2E)).astype(BF16), minus_ones],
                             axis=1) for u in range(hps)]

    def absorb(kj, m, l, u, diagonal):
        s = _nt_dot(kaug_ref[u, kj], q_aug[u])
        if diagonal:
            key = lax.broadcasted_iota(jnp.int32, (tq, tq), 0)
            qry = lax.broadcasted_iota(jnp.int32, (tq, tq), 1)
            s = jnp.where(key <= qry, s, NEG_BIG)
        m_new = jnp.maximum(m, jnp.max(s, axis=0, keepdims=True))
        alpha = jnp.exp2(m - m_new)
        p = jnp.exp2(s - m_new)
        l = alpha * l + jnp.sum(p, axis=0, keepdims=True)
        pv = jnp.dot(vt_ref[u, kj], p.astype(BF16), preferred_element_type=F32)
        acc_ref[u] = pv if kj == 0 else alpha * acc_ref[u] + pv
        return m_new, l

    for last in range(nk):
        @pl.when(qi == last)
        def _(last=last):
            for u in range(hps):
                m = jnp.full((1, tq), NEG_BIG, F32)
                l = jnp.zeros((1, tq), F32)
                for kj in range(last + 1):
                    m, l = absorb(kj, m, l, u, kj == last)
                o_ref[:, u * dh:(u + 1) * dh] = (acc_ref[u] / l).T.astype(o_ref.dtype)


def _fox_prompt(proj, c_col, *, rows, batch, seq, heads, dh, tq, hps):
    nq = seq // tq
    hb = heads // hps
    w = hps * dh
    return pl.pallas_call(
        functools.partial(_fox_prompt_kernel, tq=tq, dh=dh, scale=dh ** -0.5),
        out_shape=jax.ShapeDtypeStruct((rows, heads * dh), BF16),
        grid=(batch, hb, nq),
        in_specs=[
            pl.BlockSpec((tq, w), lambda b, h, i: (b * nq + i, h)),
            pl.BlockSpec((seq, w), lambda b, h, i: (b, hb + h)),
            pl.BlockSpec((seq, w), lambda b, h, i: (b, 2 * hb + h)),
            pl.BlockSpec((seq, c_col.shape[1]), lambda b, h, i: (b, 0)),
        ],
        out_specs=pl.BlockSpec((tq, w), lambda b, h, i: (b * nq + i, h)),
        scratch_shapes=[pltpu.VMEM((hps, nq, tq, 2 * dh), BF16),
                        pltpu.VMEM((hps, nq, dh, tq), BF16),
                        pltpu.VMEM((hps, dh, tq), F32)],
        compiler_params=_params("parallel", "parallel", "arbitrary"),
        name="fox_prompt",
    )(proj, proj, proj, c_col)


def _fox_decode_kernel(pt_ref, q_ref, kn_ref, vn_ref, lfn_ref, *refs,
                       n_pg, heads, dh, page, n_new, scale):
    lf_refs = refs[:n_pg]
    k_refs = refs[n_pg:2 * n_pg]
    v_refs = refs[2 * n_pg:3 * n_pg]
    o_ref, m_ref, l_ref, acc_ref, carry_ref = refs[3 * n_pg + 1:]
    step = pl.program_id(1)
    nt = q_ref.shape[0]
    per = heads // HEAD_GROUPS
    gr = per * nt
    cols = page * per

    @pl.when(step == 0)
    def _():
        m_ref[...] = jnp.full_like(m_ref, NEG_BIG)
        l_ref[...] = jnp.zeros_like(l_ref)
        acc_ref[...] = jnp.zeros_like(acc_ref)
        carry_ref[...] = jnp.zeros_like(carry_ref)

    grouped, carry = _past_decay(lf_refs, carry_ref[...], heads, page)
    carry_ref[...] = carry
    ck = jnp.concatenate(grouped, axis=1)

    def group_rows(ref, grp):
        return ref[pl.ds(grp, cols, stride=HEAD_GROUPS), :].astype(BF16)

    def group_queries(grp):
        parts = [q_ref[:, (grp + HEAD_GROUPS * j) * dh:(grp + HEAD_GROUPS * j + 1) * dh] for j in range(per)]
        return (jnp.concatenate(parts, axis=0) * scale).astype(BF16)

    def update(grp, s, v_blocks, width):
        m_old = m_ref[grp]
        m_new = jnp.maximum(m_old, jnp.max(s, axis=-1, keepdims=True))
        alpha = jnp.exp(m_old - m_new)
        p = jnp.exp(s - m_new)
        l_ref[grp] = alpha * l_ref[grp] + jnp.sum(p, axis=-1, keepdims=True)
        pv = None
        for i, vb in enumerate(v_blocks):
            w = p[:, i * width:(i + 1) * width].astype(BF16)
            part = jnp.dot(w, vb, preferred_element_type=F32)
            pv = part if pv is None else pv + part
        acc_ref[grp] = alpha * acc_ref[grp] + pv
        m_ref[grp] = m_new

    row = lax.broadcasted_iota(jnp.int32, (gr, n_pg * cols), 0)
    col = lax.broadcasted_iota(jnp.int32, (gr, n_pg * cols), 1)
    own = (row // nt) == lax.rem(col, per)
    for grp in range(HEAD_GROUPS):
        qg = group_queries(grp)
        s = jnp.concatenate([_nt_dot(qg, group_rows(k_refs[i], grp)) for i in range(n_pg)], axis=1)
        s = jnp.where(own, s - ck[grp:grp + 1, :], NEG_BIG)
        update(grp, s, [group_rows(v_refs[i], grp) for i in range(n_pg)], cols)

    @pl.when(step == pl.num_programs(1) - 1)
    def _():
        row = lax.broadcasted_iota(jnp.int32, (gr, V7X_LANES), 0)
        col = lax.broadcasted_iota(jnp.int32, (gr, V7X_LANES), 1)
        t_col = lax.rem(col, nt)
        visible = jnp.logical_and(jnp.logical_and((row // nt) == (col // nt), t_col <= lax.rem(row, nt)),
                                  jnp.logical_and(t_col < n_new, col < gr))
        pad = jnp.zeros((V7X_LANES - gr, dh), F32)
        cn = _new_token_decay(lfn_ref, carry, heads)
        for grp in range(HEAD_GROUPS):
            head_cols = [slice((grp + HEAD_GROUPS * j) * dh, (grp + HEAD_GROUPS * j + 1) * dh) for j in range(per)]
            k_new = jnp.concatenate([kn_ref[:, c] for c in head_cols] + [pad], axis=0).astype(BF16)
            v_new = jnp.concatenate([vn_ref[:, c] for c in head_cols] + [pad], axis=0).astype(BF16)
            s = _nt_dot(group_queries(grp), k_new)
            s = jnp.where(visible, s - cn[grp:grp + 1, :], NEG_BIG)
            update(grp, s, [v_new], V7X_LANES)
            out = acc_ref[grp] / l_ref[grp]
            for j, c in enumerate(head_cols):
                o_ref[:, c] = out[j * nt:(j + 1) * nt].astype(o_ref.dtype)


def _fox_decode(page_table, proj, logf, cache_logf, cache_k, cache_v, layer, attn, *,
                n_prompt_rows, heads, dh, n_new, n_pg):
    bd, n_pages = page_table.shape
    page = cache_logf.shape[3]
    per = heads // HEAD_GROUPS
    assert heads % HEAD_GROUPS == 0 and per * SAMPLE_ROWS <= V7X_LANES and page == V7X_LANES
    gr = per * SAMPLE_ROWS
    hd = heads * dh
    sblk = n_prompt_rows // SAMPLE_ROWS

    def row_spec(col):
        return pl.BlockSpec((SAMPLE_ROWS, hd), lambda b, g, pt: (sblk + b, col))

    def page_spec(i):
        return pl.BlockSpec((None, None, page * heads, dh),
                            lambda b, g, pt: (layer, pt[b, g * n_pg + i], 0, 0))

    def logf_page_spec(i):
        return pl.BlockSpec((None, None, heads, page),
                            lambda b, g, pt: (layer, pt[b, g * n_pg + i], 0, 0))

    attn_arg = 5 + 3 * n_pg
    grid_spec = pltpu.PrefetchScalarGridSpec(
        num_scalar_prefetch=1,
        grid=(bd, n_pages // n_pg),
        in_specs=[row_spec(0), row_spec(1), row_spec(2),
                  pl.BlockSpec((SAMPLE_ROWS, logf.shape[1]), lambda b, g, pt: (sblk + b, 0))]
        + [logf_page_spec(i) for i in range(n_pg)]
        + [page_spec(i) for i in range(n_pg)] * 2
        + [pl.BlockSpec(memory_space=pl.ANY)],
        out_specs=pl.BlockSpec((SAMPLE_ROWS, hd), lambda b, g, pt: (sblk + b, 0)),
        scratch_shapes=[pltpu.VMEM((HEAD_GROUPS, gr, 1), F32),
                        pltpu.VMEM((HEAD_GROUPS, gr, 1), F32),
                        pltpu.VMEM((HEAD_GROUPS, gr, dh), F32),
                        pltpu.VMEM((heads, 1), F32)],
    )
    return pl.pallas_call(
        functools.partial(_fox_decode_kernel, n_pg=n_pg, heads=heads, dh=dh, page=page,
                          n_new=n_new, scale=dh ** -0.5),
        out_shape=jax.ShapeDtypeStruct(attn.shape, attn.dtype),
        grid_spec=grid_spec,
        input_output_aliases={attn_arg: 0},
        compiler_params=_params("parallel", "arbitrary"),
        name="fox_decode",
    )(page_table, proj, proj, proj, logf,
      *([cache_logf] * n_pg), *([cache_k] * n_pg), *([cache_v] * n_pg), attn)


def _ret_tables(heads, chunk, t_real):
    lg = jnp.log1p(-jnp.exp2(-5.0 - jnp.arange(heads, dtype=F32)))
    n = jnp.arange(chunk, dtype=F32)
    diff = n[:, None] - n[None, :]
    causal = diff >= 0
    dmat = jnp.where(causal[None], jnp.exp(jnp.where(causal, diff, 0.0)[None] * lg[:, None, None]), 0.0)
    cross = jnp.exp((n[None, :] + 1.0) * lg[:, None])
    kdec = jnp.where(n[None, :] < t_real, jnp.exp((t_real - 1.0 - n)[None, :] * lg[:, None]), 0.0)
    full = jnp.broadcast_to(jnp.exp(t_real * lg)[:, None], (heads, chunk))
    dvec = jnp.stack([cross, kdec, full], axis=-1)
    return dmat.astype(F32), dvec.astype(F32)


def _rope_tables(positions, dk):
    half = dk // 2
    inv = ROPE_BASE ** (-jnp.arange(half, dtype=F32) / half)
    ang = positions.astype(F32)[:, None] * inv[None, :]
    return jnp.cos(ang), jnp.sin(ang)


def _ret_kernel(*refs, has_state, chunk, rows, n_chunks, dk, dv, hps):
    if has_state:
        q_ref, k_ref, v_ref, g_ref, cos_ref, sin_ref, dmat_ref, dvec_ref, s0_ref, y_ref, st_ref = refs
        st_ref[...] = s0_ref[...]
    else:
        q_ref, k_ref, v_ref, g_ref, cos_ref, sin_ref, dmat_ref, dvec_ref, y_ref, st_ref = refs
        st_ref[...] = jnp.zeros_like(st_ref)
    half = dk // 2

    def padded(x):
        x = x.astype(F32)
        if rows == chunk:
            return x
        return jnp.concatenate([x, jnp.zeros((chunk - rows, x.shape[1]), F32)], axis=0)

    tables = []
    for u in range(hps):
        dvec = dvec_ref[u]
        tables.append((dmat_ref[u], dvec[:, 0:1], dvec[:, 1:2], dvec[0:1, 2:3]))

    def head_step(c, u):
        sl = pl.ds(pl.multiple_of(c * rows, rows), rows)
        qk_cols = slice(u * dk, (u + 1) * dk)
        v_cols = slice(u * dv, (u + 1) * dv)
        dmat, cross_d, k_d, full_d = tables[u]
        cos = padded(cos_ref[sl, :])
        sin = padded(sin_ref[sl, :])

        def rope(x):
            x1 = x[:, :half]
            x2 = x[:, half:]
            return jnp.concatenate([x1 * cos - x2 * sin, x1 * sin + x2 * cos], axis=1)

        q = rope(padded(q_ref[sl, qk_cols]))
        k = rope(padded(k_ref[sl, qk_cols])) * (dk ** -0.5)
        v = padded(v_ref[sl, v_cols]).astype(BF16)
        qb = q.astype(BF16)
        s = _nt_dot(qb, k.astype(BF16)) * dmat
        state = st_ref[0, u]
        y = jnp.dot(s.astype(BF16), v, preferred_element_type=F32)
        y = y + jnp.dot(qb, state.astype(BF16), preferred_element_type=F32) * cross_d
        kdec_t = (k * k_d).T.astype(BF16)
        st_ref[0, u] = full_d * state + jnp.dot(kdec_t, v, preferred_element_type=F32)

        mu = jnp.mean(y, axis=-1, keepdims=True)
        yc = y - mu
        var = jnp.mean(yc * yc, axis=-1, keepdims=True)
        yn = yc * lax.rsqrt(var + GN_EPS)
        gate = padded(g_ref[sl, v_cols])
        out = yn * (gate * jax.nn.sigmoid(gate))
        y_ref[sl, v_cols] = out[:rows].astype(y_ref.dtype)

    def step(c, carry):
        for u in range(hps):
            head_step(c, u)
        return carry

    lax.fori_loop(0, n_chunks, step, 0, unroll=math.gcd(n_chunks, RET_UNROLL))


def _retention(proj, cos, sin, state0, layer, n_layers, y_prev, st_prev, *, row0, batch, n_chunks,
               rows, t_real, heads, dk, dv, hps):
    chunk = max(rows, RET_CHUNK)
    dmat, dvec = _ret_tables(heads, chunk, t_real)
    seq_rows = n_chunks * rows
    r0 = row0 // seq_rows
    qk_w = heads * dk
    has_state = state0 is not None

    def tok_spec(width, col0):
        cb = col0 // (hps * width)
        return pl.BlockSpec((seq_rows, hps * width), lambda b, h: (r0 + b, cb + h))

    in_specs = [
        tok_spec(dk, 0), tok_spec(dk, qk_w), tok_spec(dv, 2 * qk_w), tok_spec(dv, 2 * qk_w + heads * dv),
        pl.BlockSpec((seq_rows, dk // 2), lambda b, h: (0, 0)),
        pl.BlockSpec((seq_rows, dk // 2), lambda b, h: (0, 0)),
        pl.BlockSpec((hps, chunk, chunk), lambda b, h: (h, 0, 0)),
        pl.BlockSpec((hps, chunk, 3), lambda b, h: (h, 0, 0)),
    ]
    args = [proj, proj, proj, proj, cos, sin, dmat, dvec]
    if has_state:
        in_specs.append(pl.BlockSpec((None, 1, hps, dk, dv), lambda b, h: (layer, b, h, 0, 0)))
        args.append(state0)
    n_used = len(args)
    aliases = {}
    for prev, out_idx in ((y_prev, 0), (st_prev, 1)):
        if prev is not None:
            aliases[len(args)] = out_idx
            in_specs.append(pl.BlockSpec(memory_space=pl.ANY))
            args.append(prev)
    n_in = len(args)

    def kernel(*refs):
        refs = refs[:n_used] + refs[n_in:]
        _ret_kernel(*refs, has_state=has_state, chunk=chunk, rows=rows, n_chunks=n_chunks,
                    dk=dk, dv=dv, hps=hps)

    return pl.pallas_call(
        kernel,
        out_shape=[jax.ShapeDtypeStruct((proj.shape[0], heads * dv), BF16),
                   jax.ShapeDtypeStruct((n_layers, batch, heads, dk, dv), F32)],
        grid=(batch, heads // hps),
        in_specs=in_specs,
        out_specs=[
            pl.BlockSpec((seq_rows, hps * dv), lambda b, h: (r0 + b, h)),
            pl.BlockSpec((None, 1, hps, dk, dv), lambda b, h: (layer, b, h, 0, 0)),
        ],
        input_output_aliases=aliases,
        compiler_params=_params("parallel", "parallel"),
        name="retention",
    )(*args)


def kernel(x_prompt, x_sample, cache_fox_k, cache_fox_v, cache_fox_logf, state_ret, page_table, norm_ffn1, ffn1_w_gate, ffn1_w_up, ffn1_w_down, norm_mix, fox_w_in, fox_b_f, fox_w_out, ret_w_in, ret_w_out, norm_ffn2, ffn2_w_gate, ffn2_w_up, ffn2_w_down, norm_final):
    batch, seq, d = x_prompt.shape
    bd, n_new, _ = x_sample.shape
    depth = norm_ffn1.shape[0]
    f = ffn1_w_gate.shape[2]
    n_fox, pool, page, fox_heads, dh = cache_fox_k.shape
    ret_heads, dk, dv = state_ret.shape[2:]
    n_pages = page_table.shape[1]
    past = n_pages * page
    assert n_new <= SAMPLE_ROWS and fox_heads * dh == d and ret_heads * dk == d

    n_p = batch * seq
    n_s = bd * SAMPLE_ROWS
    rows = n_p + n_s
    tm = _tile(rows, 1100, V7X_BF16_SUBLANES)
    tf = _tile(f, V7X_MXU_DIM, V7X_LANES)
    tf_wide = _tile(f, 2 * V7X_MXU_DIM, V7X_LANES)
    tn = _tile(d, 1024, V7X_LANES)
    tn_deep = _tile(d, 512, V7X_LANES)
    tm_tall = _tile(rows, 2 * tm, tm)
    tq = _tile(seq, 1024, V7X_LANES)
    tr = _tile(n_p, 512, n_s)
    n_pg = _tile(n_pages, DECODE_PAGES_PER_STEP, 1)

    xs = jnp.pad(x_sample, ((0, 0), (0, SAMPLE_ROWS - n_new), (0, 0)))
    x = jnp.concatenate([x_prompt.reshape(n_p, d), xs.reshape(n_s, d)], axis=0)

    cache_k2 = cache_fox_k.reshape(n_fox, pool, page * fox_heads, dh)
    cache_v2 = cache_fox_v.reshape(n_fox, pool, page * fox_heads, dh)
    cache_logf_t = jnp.swapaxes(cache_fox_logf, 2, 3)
    fox_bias = jnp.pad(fox_b_f, ((0, 0), (0, V7X_LANES - fox_heads)))
    fox_w_in_t = jnp.swapaxes(fox_w_in, 1, 2)

    cos_p, sin_p = _rope_tables(jnp.arange(seq), dk)
    cos_s, sin_s = _rope_tables(past + jnp.arange(SAMPLE_ROWS), dk)
    n_ret = depth // 2

    def split_rows(a):
        w = a.shape[1]
        return (a[:n_p].reshape(batch, seq, w),
                a[n_p:].reshape(bd, SAMPLE_ROWS, w)[:, :n_new])

    ffn_stacks = ((ffn1_w_gate, ffn1_w_up, ffn1_w_down), (ffn2_w_gate, ffn2_w_up, ffn2_w_down))
    ffn_norms = (norm_ffn1, norm_ffn2)
    ffn_ready = [None]

    can_chain = d % (rows // tm) == 0 and (d // (rows // tm)) % V7X_LANES == 0

    def run_ffn(x, which, i):
        last = which == 1 and i == depth - 1
        nxt = None if (last or not can_chain) else ffn_stacks[1 - which]
        nxt_layer = i + which
        if ffn_ready[0] is None:
            weights, layer, tf_k = ffn_stacks[which], i, tf
        else:
            weights, layer, tf_k = ffn_ready[0], None, tf_wide
        x, ffn_ready[0] = _ffn(x, ffn_norms[which][i][None], weights, layer, nxt, nxt_layer, tm=tm, tf=tf_k)
        return x

    kv_p = kv_s = st_p = st_s = None
    lp_l, ls_l = [], []
    for i in range(depth):
        x = run_ffn(x, 0, i)
        j = i // 2
        g_mix = norm_mix[i][None]
        if i % 2 == 0:
            proj = _norm_matmul(x, g_mix, fox_w_in_t, j, n=3 * d, tm=tm, tn=tn, out_dtype=F32,
                                w_transposed=True)
            logf = _forget_gate(x, g_mix, fox_w_in_t, j, fox_bias[j][None], row0=3 * d, heads=fox_heads, tm=tm)
            c_col = _prompt_cumsum(logf, batch=batch, seq=seq)
            attn = _fox_prompt(proj, c_col, rows=rows, batch=batch, seq=seq, heads=fox_heads, dh=dh, tq=tq,
                               hps=math.gcd(fox_heads, FOX_HEADS_PER_STEP))
            attn = _fox_decode(page_table, proj, logf, cache_logf_t, cache_k2, cache_v2, j, attn,
                               n_prompt_rows=n_p, heads=fox_heads, dh=dh, n_new=n_new, n_pg=n_pg)
            x = _matmul_res(attn, fox_w_out, j, x, tm=tm_tall, tn=tn_deep)
            kv_p = _heads_layout(proj, kv_p, j, n_fox, row0=0, nrows=n_p, tr=tr, heads=fox_heads, dh=dh)
            kv_s = _heads_layout(proj, kv_s, j, n_fox, row0=n_p, nrows=n_s, tr=n_s, heads=fox_heads, dh=dh)
            lp, ls = split_rows(logf[:, :fox_heads])
            lp_l.append(lp)
            ls_l.append(ls)
        else:
            proj = _norm_matmul(x, g_mix, ret_w_in, j, n=ret_w_in.shape[2], tm=tm, tn=tn, out_dtype=BF16)
            y, st_p = _retention(proj, cos_p, sin_p, None, j, n_ret, None, st_p, row0=0, batch=batch,
                                 n_chunks=seq // RET_CHUNK, rows=RET_CHUNK, t_real=RET_CHUNK,
                                 heads=ret_heads, dk=dk, dv=dv, hps=math.gcd(ret_heads, RET_HEADS_PER_STEP))
            y, st_s = _retention(proj, cos_s, sin_s, state_ret, j, n_ret, y, st_s, row0=n_p, batch=bd,
                                 n_chunks=1, rows=SAMPLE_ROWS, t_real=n_new,
                                 heads=ret_heads, dk=dk, dv=dv, hps=ret_heads)
            x = _matmul_res(y, ret_w_out, j, x, tm=tm, tn=tn_deep)
        x = run_ffn(x, 1, i)

    g_fin = norm_final[None]
    y_prompt = _final_norm(x, g_fin, row0=0, nrows=n_p, tr=tr).reshape(batch, seq, d)
    y_sample = _final_norm(x, g_fin, row0=n_p, nrows=n_s, tr=n_s).reshape(bd, SAMPLE_ROWS, d)[:, :n_new]

    def heads_out(a, lead, n_tok):
        return a.reshape((n_fox,) + lead + (fox_heads, dh))[:, :, :n_tok]

    return (y_prompt, y_sample,
            heads_out(kv_p[0], (batch, seq), seq), heads_out(kv_p[1], (batch, seq), seq),
            jnp.stack(lp_l),
            heads_out(kv_s[0], (bd, SAMPLE_ROWS), n_new), heads_out(kv_s[1], (bd, SAMPLE_ROWS), n_new),
            jnp.stack(ls_l),
            st_p, st_s)
```

```python
import functools
import math

import jax
import jax.numpy as jnp
from jax import lax
from jax.experimental import pallas as pl
from jax.experimental.pallas import tpu as pltpu

F32 = jnp.float32
BF16 = jnp.bfloat16

NORM_EPS = 1e-6
GN_EPS = 1e-5
ROPE_BASE = 10000.0
NEG_BIG = -1e30
LOG2E = math.log2(math.e)

V7X_VMEM_BYTES = 64 * 1024 * 1024
V7X_LANES = 128
V7X_BF16_SUBLANES = 16
V7X_MXU_DIM = 256
VMEM_LIMIT = (V7X_VMEM_BYTES * 15) // 16

SAMPLE_ROWS = V7X_BF16_SUBLANES
RET_CHUNK = 128
RET_UNROLL = 8
N_DECAY_PIECES = 3
HEAD_GROUPS = 4
DECODE_PAGES_PER_STEP = 8
RET_HEADS_PER_STEP = 2
FOX_HEADS_PER_STEP = 4


def _tile(n, target, align):
    for t in range(min(n, target), 0, -1):
        if n % t == 0 and t % align == 0:
            return t
    raise ValueError(f"no tile for {n} (target {target}, align {align})")


def _params(*sem):
    return pltpu.CompilerParams(dimension_semantics=sem, vmem_limit_bytes=VMEM_LIMIT)


def _rms_norm(x, g):
    ms = jnp.mean(x * x, axis=-1, keepdims=True)
    return (x * lax.rsqrt(ms + NORM_EPS)) * g


def _nt_dot(a, b):
    return lax.dot_general(a, b, (((1,), (1,)), ((), ())), preferred_element_type=F32)


def _ffn_kernel(x_ref, g_ref, wg_ref, wu_ref, wd_ref, *refs, tf, cast_next):
    if cast_next:
        (ng_ref, nu_ref, nd_ref), refs = refs[:3], refs[3:]
        o_ref, og_ref, ou_ref, od_ref, h_ref = refs
        og_ref[...] = ng_ref[...].astype(BF16)
        ou_ref[...] = nu_ref[...].astype(BF16)
        od_ref[...] = nd_ref[...].astype(BF16)
    else:
        o_ref, h_ref = refs

    @pl.when(pl.program_id(1) == 0)
    def _():
        x = x_ref[...]
        h_ref[...] = _rms_norm(x, g_ref[...]).astype(BF16)
        o_ref[...] = x

    wgu = jnp.concatenate([wg_ref[...].astype(BF16), wu_ref[...].astype(BF16)], axis=1)
    ab = jnp.dot(h_ref[...], wgu, preferred_element_type=F32)
    a = ab[:, :tf]
    b = ab[:, tf:]
    p = (a * jax.nn.sigmoid(a)) * (0.5 * b)
    o_ref[...] += jnp.dot(p.astype(BF16), wd_ref[...].astype(BF16), preferred_element_type=F32)


def _ffn(x, g, weights, layer, next_weights, next_layer, *, tm, tf):
    r, d = x.shape
    w_gate, w_up, w_down = weights
    f = w_down.shape[-2]
    ni, nj = r // tm, f // tf
    if layer is None:
        w_specs = [pl.BlockSpec((d, tf), lambda i, j: (0, j)),
                   pl.BlockSpec((d, tf), lambda i, j: (0, j)),
                   pl.BlockSpec((tf, d), lambda i, j: (j, 0))]
    else:
        w_specs = [pl.BlockSpec((None, d, tf), lambda i, j: (layer, 0, j)),
                   pl.BlockSpec((None, d, tf), lambda i, j: (layer, 0, j)),
                   pl.BlockSpec((None, tf, d), lambda i, j: (layer, j, 0))]
    in_specs = [pl.BlockSpec((tm, d), lambda i, j: (i, 0)), pl.BlockSpec((1, d), lambda i, j: (0, 0))] + w_specs
    args = [x, g, w_gate, w_up, w_down]
    out_specs = [pl.BlockSpec((tm, d), lambda i, j: (i, 0))]
    out_shape = [jax.ShapeDtypeStruct((r, d), F32)]
    cast_next = next_weights is not None
    if cast_next:
        assert d % ni == 0 and (d // ni) % V7X_LANES == 0
        dr = d // ni
        in_specs += [pl.BlockSpec((None, dr, tf), lambda i, j: (next_layer, i, j)),
                     pl.BlockSpec((None, dr, tf), lambda i, j: (next_layer, i, j)),
                     pl.BlockSpec((None, tf, dr), lambda i, j: (next_layer, j, i))]
        args += list(next_weights)
        out_specs += [pl.BlockSpec((dr, tf), lambda i, j: (i, j)),
                      pl.BlockSpec((dr, tf), lambda i, j: (i, j)),
                      pl.BlockSpec((tf, dr), lambda i, j: (j, i))]
        out_shape += [jax.ShapeDtypeStruct((d, f), BF16), jax.ShapeDtypeStruct((d, f), BF16),
                      jax.ShapeDtypeStruct((f, d), BF16)]
    res = pl.pallas_call(
        functools.partial(_ffn_kernel, tf=tf, cast_next=cast_next),
        out_shape=out_shape,
        grid=(ni, nj),
        in_specs=in_specs,
        out_specs=out_specs,
        scratch_shapes=[pltpu.VMEM((tm, d), BF16)],
        compiler_params=_params("parallel", "arbitrary"),
        name="ffn",
    )(*args)
    return res[0], (tuple(res[1:]) if cast_next else None)


def _norm_matmul_kernel(x_ref, g_ref, w_ref, o_ref, h_ref, *, w_transposed):
    @pl.when(pl.program_id(1) == 0)
    def _():
        h_ref[...] = _rms_norm(x_ref[...], g_ref[...]).astype(BF16)

    w = w_ref[...].astype(BF16)
    if w_transposed:
        out = _nt_dot(h_ref[...], w)
    else:
        out = jnp.dot(h_ref[...], w, preferred_element_type=F32)
    o_ref[...] = out.astype(o_ref.dtype)


def _norm_matmul(x, g, w, layer, *, n, tm, tn, out_dtype, w_transposed=False):
    r, d = x.shape
    if w_transposed:
        w_spec = pl.BlockSpec((None, tn, d), lambda i, j: (layer, j, 0))
    else:
        w_spec = pl.BlockSpec((None, d, tn), lambda i, j: (layer, 0, j))
    return pl.pallas_call(
        functools.partial(_norm_matmul_kernel, w_transposed=w_transposed),
        out_shape=jax.ShapeDtypeStruct((r, n), out_dtype),
        grid=(r // tm, n // tn),
        in_specs=[
            pl.BlockSpec((tm, d), lambda i, j: (i, 0)),
            pl.BlockSpec((1, d), lambda i, j: (0, 0)),
            w_spec,
        ],
        out_specs=pl.BlockSpec((tm, tn), lambda i, j: (i, j)),
        scratch_shapes=[pltpu.VMEM((tm, d), BF16)],
        compiler_params=_params("parallel", "arbitrary"),
        name="norm_matmul",
    )(x, g, w)


def _forget_gate_kernel(x_ref, g_ref, w_ref, b_ref, o_ref, *, heads):
    h = _rms_norm(x_ref[...], g_ref[...]).astype(BF16)
    row = lax.broadcasted_iota(jnp.int32, w_ref.shape, 0)
    w = jnp.where(row < heads, w_ref[...], 0.0).astype(BF16)
    z = _nt_dot(h, w) + b_ref[...]
    o_ref[...] = -(jnp.maximum(-z, 0.0) + jnp.log1p(jnp.exp(-jnp.abs(z))))


def _forget_gate(x, g, w_t, layer, b_f, *, row0, heads, tm):
    r, d = x.shape
    n = V7X_LANES
    cb = row0 // n
    return pl.pallas_call(
        functools.partial(_forget_gate_kernel, heads=heads),
        out_shape=jax.ShapeDtypeStruct((r, n), F32),
        grid=(r // tm,),
        in_specs=[
            pl.BlockSpec((tm, d), lambda i: (i, 0)),
            pl.BlockSpec((1, d), lambda i: (0, 0)),
            pl.BlockSpec((None, n, d), lambda i: (layer, cb, 0)),
            pl.BlockSpec((1, n), lambda i: (0, 0)),
        ],
        out_specs=pl.BlockSpec((tm, n), lambda i: (i, 0)),
        compiler_params=_params("parallel"),
        name="forget_gate",
    )(x, g, w_t, b_f)


def _matmul_res_kernel(a_ref, w_ref, r_ref, o_ref):
    w = w_ref[...].astype(BF16)
    o_ref[...] = r_ref[...] + jnp.dot(a_ref[...], w, preferred_element_type=F32)


def _matmul_res(a, w, layer, res, *, tm, tn):
    r, k = a.shape
    n = w.shape[2]
    return pl.pallas_call(
        _matmul_res_kernel,
        out_shape=jax.ShapeDtypeStruct((r, n), F32),
        grid=(r // tm, n // tn),
        in_specs=[
            pl.BlockSpec((tm, k), lambda i, j: (i, 0)),
            pl.BlockSpec((None, k, tn), lambda i, j: (layer, 0, j)),
            pl.BlockSpec((tm, tn), lambda i, j: (i, j)),
        ],
        out_specs=pl.BlockSpec((tm, tn), lambda i, j: (i, j)),
        compiler_params=_params("parallel", "arbitrary"),
        name="matmul_res",
    )(a, w, res)


def _final_norm_kernel(x_ref, g_ref, o_ref):
    o_ref[...] = _rms_norm(x_ref[...], g_ref[...])


def _final_norm(x, g, *, row0, nrows, tr):
    d = x.shape[1]
    b0 = row0 // tr
    return pl.pallas_call(
        _final_norm_kernel,
        out_shape=jax.ShapeDtypeStruct((nrows, d), F32),
        grid=(nrows // tr,),
        in_specs=[pl.BlockSpec((tr, d), lambda i: (b0 + i, 0)), pl.BlockSpec((1, d), lambda i: (0, 0))],
        out_specs=pl.BlockSpec((tr, d), lambda i: (i, 0)),
        compiler_params=_params("parallel"),
        name="final_norm",
    )(x, g)


def _heads_layout_kernel(*refs, heads, dh, aliased):
    k_ref, v_ref = refs[:2]
    ko_ref, vo_ref = refs[2 + 2 * aliased:]
    tr = k_ref.shape[0]
    for src, dst in ((k_ref, ko_ref), (v_ref, vo_ref)):
        for h in range(heads):
            dst[pl.ds(h, tr, stride=heads), :] = src[:, h * dh:(h + 1) * dh]


def _heads_layout(proj, prev, layer, n_layers, *, row0, nrows, tr, heads, dh):
    d = heads * dh
    b0 = row0 // tr
    aliased = prev is not None
    shape = jax.ShapeDtypeStruct((n_layers, nrows * heads, dh), F32)
    in_specs = [pl.BlockSpec((tr, d), lambda i: (b0 + i, 1)),
                pl.BlockSpec((tr, d), lambda i: (b0 + i, 2))]
    args = [proj, proj]
    if aliased:
        in_specs += [pl.BlockSpec(memory_space=pl.ANY)] * 2
        args += list(prev)
    out_spec = pl.BlockSpec((None, tr * heads, dh), lambda i: (layer, i, 0))
    return pl.pallas_call(
        functools.partial(_heads_layout_kernel, heads=heads, dh=dh, aliased=aliased),
        out_shape=[shape, shape],
        grid=(nrows // tr,),
        in_specs=in_specs,
        out_specs=[out_spec, out_spec],
        input_output_aliases={2: 0, 3: 1} if aliased else {},
        compiler_params=_params("parallel"),
        name="heads_layout",
    )(*args)


def _split_bf16(x):
    pieces = []
    rem = x
    for _ in range(N_DECAY_PIECES):
        p = rem.astype(BF16)
        pieces.append(p)
        rem = rem - p.astype(F32)
    return pieces


def _tri_ones(n, lower):
    row = lax.broadcasted_iota(jnp.int32, (n, n), 0)
    col = lax.broadcasted_iota(jnp.int32, (n, n), 1)
    keep = (col <= row) if lower else (row <= col)
    return jnp.where(keep, 1.0, 0.0).astype(BF16)


def _prompt_cumsum_kernel(lf_ref, c_ref, *, blk):
    t = lf_ref.shape[0]
    lower = _tri_ones(blk, lower=True)
    carry = jnp.zeros((1, lf_ref.shape[1]), F32)
    for i in range(t // blk):
        local = carry
        for piece in _split_bf16(lf_ref[i * blk:(i + 1) * blk, :]):
            local = local + jnp.dot(lower, piece, preferred_element_type=F32)
        c_ref[i * blk:(i + 1) * blk, :] = local
        carry = local[blk - 1:blk, :]


def _prompt_cumsum(logf, *, batch, seq):
    w = logf.shape[1]
    return pl.pallas_call(
        functools.partial(_prompt_cumsum_kernel, blk=V7X_LANES),
        out_shape=jax.ShapeDtypeStruct((batch * seq, w), F32),
        grid=(batch,),
        in_specs=[pl.BlockSpec((seq, w), lambda b: (b, 0))],
        out_specs=pl.BlockSpec((seq, w), lambda b: (b, 0)),
        compiler_params=_params("parallel"),
        name="prompt_cumsum",
    )(logf)


def _lane_cumsum_block(xt, upper):
    out = None
    for piece in _split_bf16(xt):
        part = jnp.dot(piece, upper, preferred_element_type=F32)
        out = part if out is None else out + part
    return out


def _regroup(c, spread, valid, heads):
    y = None
    for piece in _split_bf16(c):
        part = jnp.dot(piece, spread, preferred_element_type=F32)
        y = part if y is None else y + part
    out = []
    for i in range(c.shape[0] // heads):
        blk = y[i * heads:(i + 1) * heads]
        rows = [jnp.sum(jnp.where(v, blk, 0.0), axis=0, keepdims=True) for v in valid]
        out.append(jnp.concatenate(rows, axis=0))
    return out


def _group_masks(heads, m, head_slot):
    head = lax.broadcasted_iota(jnp.int32, (heads, m), 0)
    col = lax.broadcasted_iota(jnp.int32, (heads, m), 1)
    owns = (head // HEAD_GROUPS) == head_slot(col)
    return [jnp.logical_and(lax.rem(head, HEAD_GROUPS) == grp, owns) for grp in range(HEAD_GROUPS)]


def _past_decay(page_refs, carry, heads, page):
    n_pg = len(page_refs)
    per = heads // HEAD_GROUPS
    nr = n_pg * heads
    m_past = page * per
    t_idx = lax.broadcasted_iota(jnp.int32, (page, m_past), 0)
    c_idx = lax.broadcasted_iota(jnp.int32, (page, m_past), 1)
    spread = jnp.where(c_idx // per == t_idx, 1.0, 0.0).astype(BF16)
    upper = _tri_ones(page, lower=False)

    x = jnp.concatenate([ref[...] for ref in page_refs], axis=0)
    within = _lane_cumsum_block(x, upper)
    totals = jnp.broadcast_to(within[:, page - 1:page], (nr, page))
    r_idx = lax.broadcasted_iota(jnp.int32, (nr, nr), 0)
    q_idx = lax.broadcasted_iota(jnp.int32, (nr, nr), 1)
    earlier_page = jnp.logical_and(lax.rem(r_idx, heads) == lax.rem(q_idx, heads), q_idx < r_idx)
    earlier_page = jnp.where(earlier_page, 1.0, 0.0).astype(BF16)
    offset = None
    for piece in _split_bf16(totals):
        part = jnp.dot(earlier_page, piece, preferred_element_type=F32)
        offset = part if offset is None else offset + part
    local = within + offset + jnp.concatenate([carry] * n_pg, axis=0)
    grouped = _regroup(local, spread, _group_masks(heads, m_past, lambda col: lax.rem(col, per)), heads)
    return grouped, local[nr - heads:, page - 1:page]


def _new_token_decay(new_ref, carry, heads):
    nt = new_ref.shape[0]
    per = heads // HEAD_GROUPS
    xn = jnp.concatenate([new_ref[...], jnp.zeros((V7X_LANES - nt, new_ref.shape[1]), F32)], axis=0)
    new = _lane_cumsum_block(xn.T[:heads, :], _tri_ones(V7X_LANES, lower=False)) + carry
    t_new = lax.broadcasted_iota(jnp.int32, (V7X_LANES, V7X_LANES), 0)
    c_new = lax.broadcasted_iota(jnp.int32, (V7X_LANES, V7X_LANES), 1)
    hit = jnp.logical_and(lax.rem(c_new, nt) == t_new, c_new < per * nt)
    return _regroup(new, jnp.where(hit, 1.0, 0.0).astype(BF16),
                    _group_masks(heads, V7X_LANES, lambda col: col // nt), heads)[0]


def _fox_prompt_kernel(q_ref, k_ref, v_ref, c_ref, o_ref, kaug_ref, vt_ref, acc_ref, *, tq, dh, scale):
    qi = pl.program_id(2)
    hps, nk = vt_ref.shape[0], vt_ref.shape[1]
    lanes = c_ref.shape[1]

    @pl.when(qi == 0)
    def _():
        row = lax.broadcasted_iota(jnp.int32, (lanes, dh), 0)
        col = lax.broadcasted_iota(jnp.int32, (lanes, dh), 1)
        for j in range(nk):
            sl = slice(j * tq, (j + 1) * tq)
            pieces = _split_bf16(c_ref[sl, :] * LOG2E)
            for u in range(hps):
                head = pl.program_id(1) * hps + u
                hc = slice(u * dh, (u + 1) * dh)
                aug = jnp.zeros((tq, dh), F32)
                for i, piece in enumerate(pieces):
                    place = jnp.where(jnp.logical_and(row == head, col == i), 1.0, 0.0).astype(BF16)
                    aug = aug + jnp.dot(piece, place, preferred_element_type=F32)
                kaug_ref[u, j, :, :dh] = k_ref[sl, hc].astype(BF16)
                kaug_ref[u, j, :, dh:] = aug.astype(BF16)
                vt_ref[u, j] = v_ref[sl, hc].T.astype(BF16)

    lane = lax.broadcasted_iota(jnp.int32, (tq, dh), 1)
    minus_ones = jnp.where(lane < N_DECAY_PIECES, -1.0, 0.0).astype(BF16)
    q_aug = [jnp.concatenate([(q_ref[:, u * dh:(u + 1) * dh] * (scale * LOG2E)).astype(BF16), minus_ones],
                             axis=1) for u in range(hps)]

    def absorb(kj, m, l, u, diagonal):
        s = _nt_dot(kaug_ref[u, kj], q_aug[u])
        if diagonal:
            key = lax.broadcasted_iota(jnp.int32, (tq, tq), 0)
            qry = lax.broadcasted_iota(jnp.int32, (tq, tq), 1)
            s = jnp.where(key <= qry, s, NEG_BIG)
        m_new = jnp.maximum(m, jnp.max(s, axis=0, keepdims=True))
        alpha = jnp.exp2(m - m_new)
        p = jnp.exp2(s - m_new)
        l = alpha * l + jnp.sum(p, axis=0, keepdims=True)
        pv = jnp.dot(vt_ref[u, kj], p.astype(BF16), preferred_element_type=F32)
        acc_ref[u] = pv if kj == 0 else alpha * acc_ref[u] + pv
        return m_new, l

    for last in range(nk):
        @pl.when(qi == last)
        def _(last=last):
            for u in range(hps):
                m = jnp.full((1, tq), NEG_BIG, F32)
                l = jnp.zeros((1, tq), F32)
                for kj in range(last + 1):
                    m, l = absorb(kj, m, l, u, kj == last)
                o_ref[:, u * dh:(u + 1) * dh] = (acc_ref[u] / l).T.astype(o_ref.dtype)


def _fox_prompt(proj, c_col, *, rows, batch, seq, heads, dh, tq, hps):
    nq = seq // tq
    hb = heads // hps
    w = hps * dh
    return pl.pallas_call(
        functools.partial(_fox_prompt_kernel, tq=tq, dh=dh, scale=dh ** -0.5),
        out_shape=jax.ShapeDtypeStruct((rows, heads * dh), BF16),
        grid=(batch, hb, nq),
        in_specs=[
            pl.BlockSpec((tq, w), lambda b, h, i: (b * nq + i, h)),
            pl.BlockSpec((seq, w), lambda b, h, i: (b, hb + h)),
            pl.BlockSpec((seq, w), lambda b, h, i: (b, 2 * hb + h)),
            pl.BlockSpec((seq, c_col.shape[1]), lambda b, h, i: (b, 0)),
        ],
        out_specs=pl.BlockSpec((tq, w), lambda b, h, i: (b * nq + i, h)),
        scratch_shapes=[pltpu.VMEM((hps, nq, tq, 2 * dh), BF16),
                        pltpu.VMEM((hps, nq, dh, tq), BF16),
                        pltpu.VMEM((hps, dh, tq), F32)],
        compiler_params=_params("parallel", "parallel", "arbitrary"),
        name="fox_prompt",
    )(proj, proj, proj, c_col)


def _fox_decode_kernel(pt_ref, q_ref, kn_ref, vn_ref, lfn_ref, *refs,
                       n_pg, heads, dh, page, n_new, scale):
    lf_refs = refs[:n_pg]
    k_refs = refs[n_pg:2 * n_pg]
    v_refs = refs[2 * n_pg:3 * n_pg]
    o_ref, m_ref, l_ref, acc_ref, carry_ref = refs[3 * n_pg + 1:]
    step = pl.program_id(1)
    nt = q_ref.shape[0]
    per = heads // HEAD_GROUPS
    gr = per * nt
    cols = page * per

    @pl.when(step == 0)
    def _():
        m_ref[...] = jnp.full_like(m_ref, NEG_BIG)
        l_ref[...] = jnp.zeros_like(l_ref)
        acc_ref[...] = jnp.zeros_like(acc_ref)
        carry_ref[...] = jnp.zeros_like(carry_ref)

    grouped, carry = _past_decay(lf_refs, carry_ref[...], heads, page)
    carry_ref[...] = carry
    ck = jnp.concatenate(grouped, axis=1)

    def group_rows(ref, grp):
        return ref[pl.ds(grp, cols, stride=HEAD_GROUPS), :].astype(BF16)

    def group_queries(grp):
        parts = [q_ref[:, (grp + HEAD_GROUPS * j) * dh:(grp + HEAD_GROUPS * j + 1) * dh] for j in range(per)]
        return (jnp.concatenate(parts, axis=0) * scale).astype(BF16)

    def update(grp, s, v_blocks, width):
        m_old = m_ref[grp]
        m_new = jnp.maximum(m_old, jnp.max(s, axis=-1, keepdims=True))
        alpha = jnp.exp(m_old - m_new)
        p = jnp.exp(s - m_new)
        l_ref[grp] = alpha * l_ref[grp] + jnp.sum(p, axis=-1, keepdims=True)
        pv = None
        for i, vb in enumerate(v_blocks):
            w = p[:, i * width:(i + 1) * width].astype(BF16)
            part = jnp.dot(w, vb, preferred_element_type=F32)
            pv = part if pv is None else pv + part
        acc_ref[grp] = alpha * acc_ref[grp] + pv
        m_ref[grp] = m_new

    row = lax.broadcasted_iota(jnp.int32, (gr, n_pg * cols), 0)
    col = lax.broadcasted_iota(jnp.int32, (gr, n_pg * cols), 1)
    own = (row // nt) == lax.rem(col, per)
    for grp in range(HEAD_GROUPS):
        qg = group_queries(grp)
        s = jnp.concatenate([_nt_dot(qg, group_rows(k_refs[i], grp)) for i in range(n_pg)], axis=1)
        s = jnp.where(own, s - ck[grp:grp + 1, :], NEG_BIG)
        update(grp, s, [group_rows(v_refs[i], grp) for i in range(n_pg)], cols)

    @pl.when(step == pl.num_programs(1) - 1)
    def _():
        row = lax.broadcasted_iota(jnp.int32, (gr, V7X_LANES), 0)
        col = lax.broadcasted_iota(jnp.int32, (gr, V7X_LANES), 1)
        t_col = lax.rem(col, nt)
        visible = jnp.logical_and(jnp.logical_and((row // nt) == (col // nt), t_col <= lax.rem(row, nt)),
                                  jnp.logical_and(t_col < n_new, col < gr))
        pad = jnp.zeros((V7X_LANES - gr, dh), F32)
        cn = _new_token_decay(lfn_ref, carry, heads)
        for grp in range(HEAD_GROUPS):
            head_cols = [slice((grp + HEAD_GROUPS * j) * dh, (grp + HEAD_GROUPS * j + 1) * dh) for j in range(per)]
            k_new = jnp.concatenate([kn_ref[:, c] for c in head_cols] + [pad], axis=0).astype(BF16)
            v_new = jnp.concatenate([vn_ref[:, c] for c in head_cols] + [pad], axis=0).astype(BF16)
            s = _nt_dot(group_queries(grp), k_new)
            s = jnp.where(visible, s - cn[grp:grp + 1, :], NEG_BIG)
            update(grp, s, [v_new], V7X_LANES)
            out = acc_ref[grp] / l_ref[grp]
            for j, c in enumerate(head_cols):
                o_ref[:, c] = out[j * nt:(j + 1) * nt].astype(o_ref.dtype)


def _fox_decode(page_table, proj, logf, cache_logf, cache_k, cache_v, layer, attn, *,
                n_prompt_rows, heads, dh, n_new, n_pg):
    bd, n_pages = page_table.shape
    page = cache_logf.shape[3]
    per = heads // HEAD_GROUPS
    assert heads % HEAD_GROUPS == 0 and per * SAMPLE_ROWS <= V7X_LANES and page == V7X_LANES
    gr = per * SAMPLE_ROWS
    hd = heads * dh
    sblk = n_prompt_rows // SAMPLE_ROWS

    def row_spec(col):
        return pl.BlockSpec((SAMPLE_ROWS, hd), lambda b, g, pt: (sblk + b, col))

    def page_spec(i):
        return pl.BlockSpec((None, None, page * heads, dh),
                            lambda b, g, pt: (layer, pt[b, g * n_pg + i], 0, 0))

    def logf_page_spec(i):
        return pl.BlockSpec((None, None, heads, page),
                            lambda b, g, pt: (layer, pt[b, g * n_pg + i], 0, 0))

    attn_arg = 5 + 3 * n_pg
    grid_spec = pltpu.PrefetchScalarGridSpec(
        num_scalar_prefetch=1,
        grid=(bd, n_pages // n_pg),
        in_specs=[row_spec(0), row_spec(1), row_spec(2),
                  pl.BlockSpec((SAMPLE_ROWS, logf.shape[1]), lambda b, g, pt: (sblk + b, 0))]
        + [logf_page_spec(i) for i in range(n_pg)]
        + [page_spec(i) for i in range(n_pg)] * 2
        + [pl.BlockSpec(memory_space=pl.ANY)],
        out_specs=pl.BlockSpec((SAMPLE_ROWS, hd), lambda b, g, pt: (sblk + b, 0)),
        scratch_shapes=[pltpu.VMEM((HEAD_GROUPS, gr, 1), F32),
                        pltpu.VMEM((HEAD_GROUPS, gr, 1), F32),
                        pltpu.VMEM((HEAD_GROUPS, gr, dh), F32),
                        pltpu.VMEM((heads, 1), F32)],
    )
    return pl.pallas_call(
        functools.partial(_fox_decode_kernel, n_pg=n_pg, heads=heads, dh=dh, page=page,
                          n_new=n_new, scale=dh ** -0.5),
        out_shape=jax.ShapeDtypeStruct(attn.shape, attn.dtype),
        grid_spec=grid_spec,
        input_output_aliases={attn_arg: 0},
        compiler_params=_params("parallel", "arbitrary"),
        name="fox_decode",
    )(page_table, proj, proj, proj, logf,
      *([cache_logf] * n_pg), *([cache_k] * n_pg), *([cache_v] * n_pg), attn)


def _ret_tables(heads, chunk, t_real):
    lg = jnp.log1p(-jnp.exp2(-5.0 - jnp.arange(heads, dtype=F32)))
    n = jnp.arange(chunk, dtype=F32)
    diff = n[:, None] - n[None, :]
    causal = diff >= 0
    dmat = jnp.where(causal[None], jnp.exp(jnp.where(causal, diff, 0.0)[None] * lg[:, None, None]), 0.0)
    cross = jnp.exp((n[None, :] + 1.0) * lg[:, None])
    kdec = jnp.where(n[None, :] < t_real, jnp.exp((t_real - 1.0 - n)[None, :] * lg[:, None]), 0.0)
    full = jnp.broadcast_to(jnp.exp(t_real * lg)[:, None], (heads, chunk))
    dvec = jnp.stack([cross, kdec, full], axis=-1)
    return dmat.astype(F32), dvec.astype(F32)


def _rope_tables(positions, dk):
    half = dk // 2
    inv = ROPE_BASE ** (-jnp.arange(half, dtype=F32) / half)
    ang = positions.astype(F32)[:, None] * inv[None, :]
    return jnp.cos(ang), jnp.sin(ang)


def _ret_kernel(*refs, has_state, chunk, rows, n_chunks, dk, dv, hps):
    if has_state:
        q_ref, k_ref, v_ref, g_ref, cos_ref, sin_ref, dmat_ref, dvec_ref, s0_ref, y_ref, st_ref = refs
        st_ref[...] = s0_ref[...]
    else:
        q_ref, k_ref, v_ref, g_ref, cos_ref, sin_ref, dmat_ref, dvec_ref, y_ref, st_ref = refs
        st_ref[...] = jnp.zeros_like(st_ref)
    half = dk // 2

    def padded(x):
        x = x.astype(F32)
        if rows == chunk:
            return x
        return jnp.concatenate([x, jnp.zeros((chunk - rows, x.shape[1]), F32)], axis=0)

    tables = []
    for u in range(hps):
        dvec = dvec_ref[u]
        tables.append((dmat_ref[u], dvec[:, 0:1], dvec[:, 1:2], dvec[0:1, 2:3]))

    def head_step(c, u):
        sl = pl.ds(pl.multiple_of(c * rows, rows), rows)
        qk_cols = slice(u * dk, (u + 1) * dk)
        v_cols = slice(u * dv, (u + 1) * dv)
        dmat, cross_d, k_d, full_d = tables[u]
        cos = padded(cos_ref[sl, :])
        sin = padded(sin_ref[sl, :])

        def rope(x):
            x1 = x[:, :half]
            x2 = x[:, half:]
            return jnp.concatenate([x1 * cos - x2 * sin, x1 * sin + x2 * cos], axis=1)

        q = rope(padded(q_ref[sl, qk_cols]))
        k = rope(padded(k_ref[sl, qk_cols])) * (dk ** -0.5)
        v = padded(v_ref[sl, v_cols]).astype(BF16)
        qb = q.astype(BF16)
        s = _nt_dot(qb, k.astype(BF16)) * dmat
        state = st_ref[0, u]
        y = jnp.dot(s.astype(BF16), v, preferred_element_type=F32)
        y = y + jnp.dot(qb, state.astype(BF16), preferred_element_type=F32) * cross_d
        kdec_t = (k * k_d).T.astype(BF16)
        st_ref[0, u] = full_d * state + jnp.dot(kdec_t, v, preferred_element_type=F32)

        mu = jnp.mean(y, axis=-1, keepdims=True)
        yc = y - mu
        var = jnp.mean(yc * yc, axis=-1, keepdims=True)
        yn = yc * lax.rsqrt(var + GN_EPS)
        gate = padded(g_ref[sl, v_cols])
        out = yn * (gate * jax.nn.sigmoid(gate))
        y_ref[sl, v_cols] = out[:rows].astype(y_ref.dtype)

    def step(c, carry):
        for u in range(hps):
            head_step(c, u)
        return carry

    lax.fori_loop(0, n_chunks, step, 0, unroll=math.gcd(n_chunks, RET_UNROLL))


def _retention(proj, cos, sin, state0, layer, n_layers, y_prev, st_prev, *, row0, batch, n_chunks,
               rows, t_real, heads, dk, dv, hps):
    chunk = max(rows, RET_CHUNK)
    dmat, dvec = _ret_tables(heads, chunk, t_real)
    seq_rows = n_chunks * rows
    r0 = row0 // seq_rows
    qk_w = heads * dk
    has_state = state0 is not None

    def tok_spec(width, col0):
        cb = col0 // (hps * width)
        return pl.BlockSpec((seq_rows, hps * width), lambda b, h: (r0 + b, cb + h))

    in_specs = [
        tok_spec(dk, 0), tok_spec(dk, qk_w), tok_spec(dv, 2 * qk_w), tok_spec(dv, 2 * qk_w + heads * dv),
        pl.BlockSpec((seq_rows, dk // 2), lambda b, h: (0, 0)),
        pl.BlockSpec((seq_rows, dk // 2), lambda b, h: (0, 0)),
        pl.BlockSpec((hps, chunk, chunk), lambda b, h: (h, 0, 0)),
        pl.BlockSpec((hps, chunk, 3), lambda b, h: (h, 0, 0)),
    ]
    args = [proj, proj, proj, proj, cos, sin, dmat, dvec]
    if has_state:
        in_specs.append(pl.BlockSpec((None, 1, hps, dk, dv), lambda b, h: (layer, b, h, 0, 0)))
        args.append(state0)
    n_used = len(args)
    aliases = {}
    for prev, out_idx in ((y_prev, 0), (st_prev, 1)):
        if prev is not None:
            aliases[len(args)] = out_idx
            in_specs.append(pl.BlockSpec(memory_space=pl.ANY))
            args.append(prev)
    n_in = len(args)

    def kernel(*refs):
        refs = refs[:n_used] + refs[n_in:]
        _ret_kernel(*refs, has_state=has_state, chunk=chunk, rows=rows, n_chunks=n_chunks,
                    dk=dk, dv=dv, hps=hps)

    return pl.pallas_call(
        kernel,
        out_shape=[jax.ShapeDtypeStruct((proj.shape[0], heads * dv), BF16),
                   jax.ShapeDtypeStruct((n_layers, batch, heads, dk, dv), F32)],
        grid=(batch, heads // hps),
        in_specs=in_specs,
        out_specs=[
            pl.BlockSpec((seq_rows, hps * dv), lambda b, h: (r0 + b, h)),
            pl.BlockSpec((None, 1, hps, dk, dv), lambda b, h: (layer, b, h, 0, 0)),
        ],
        input_output_aliases=aliases,
        compiler_params=_params("parallel", "parallel"),
        name="retention",
    )(*args)


def kernel(x_prompt, x_sample, cache_fox_k, cache_fox_v, cache_fox_logf, state_ret, page_table, norm_ffn1, ffn1_w_gate, ffn1_w_up, ffn1_w_down, norm_mix, fox_w_in, fox_b_f, fox_w_out, ret_w_in, ret_w_out, norm_ffn2, ffn2_w_gate, ffn2_w_up, ffn2_w_down, norm_final):
    batch, seq, d = x_prompt.shape
    bd, n_new, _ = x_sample.shape
    depth = norm_ffn1.shape[0]
    f = ffn1_w_gate.shape[2]
    n_fox, pool, page, fox_heads, dh = cache_fox_k.shape
    ret_heads, dk, dv = state_ret.shape[2:]
    n_pages = page_table.shape[1]
    past = n_pages * page
    assert n_new <= SAMPLE_ROWS and fox_heads * dh == d and ret_heads * dk == d

    n_p = batch * seq
    n_s = bd * SAMPLE_ROWS
    rows = n_p + n_s
    tm = _tile(rows, 1100, V7X_BF16_SUBLANES)
    tf = _tile(f, V7X_MXU_DIM, V7X_LANES)
    tf_wide = _tile(f, 2 * V7X_MXU_DIM, V7X_LANES)
    tn = _tile(d, 1024, V7X_LANES)
    tn_deep = _tile(d, 512, V7X_LANES)
    tn_wide = _tile(ret_w_in.shape[2], 1536, V7X_LANES)
    tm_tall = _tile(rows, 2 * tm, tm)
    tq = _tile(seq, 1024, V7X_LANES)
    tr = _tile(n_p, 512, n_s)
    n_pg = _tile(n_pages, DECODE_PAGES_PER_STEP, 1)

    xs = jnp.pad(x_sample, ((0, 0), (0, SAMPLE_ROWS - n_new), (0, 0)))
    x = jnp.concatenate([x_prompt.reshape(n_p, d), xs.reshape(n_s, d)], axis=0)

    cache_k2 = cache_fox_k.reshape(n_fox, pool, page * fox_heads, dh)
    cache_v2 = cache_fox_v.reshape(n_fox, pool, page * fox_heads, dh)
    cache_logf_t = jnp.swapaxes(cache_fox_logf, 2, 3)
    fox_bias = jnp.pad(fox_b_f, ((0, 0), (0, V7X_LANES - fox_heads)))
    fox_w_in_t = jnp.swapaxes(fox_w_in, 1, 2)

    cos_p, sin_p = _rope_tables(jnp.arange(seq), dk)
    cos_s, sin_s = _rope_tables(past + jnp.arange(SAMPLE_ROWS), dk)
    n_ret = depth // 2

    def split_rows(a):
        w = a.shape[1]
        return (a[:n_p].reshape(batch, seq, w),
                a[n_p:].reshape(bd, SAMPLE_ROWS, w)[:, :n_new])

    ffn_stacks = ((ffn1_w_gate, ffn1_w_up, ffn1_w_down), (ffn2_w_gate, ffn2_w_up, ffn2_w_down))
    ffn_norms = (norm_ffn1, norm_ffn2)
    ffn_ready = [None]

    can_chain = d % (rows // tm) == 0 and (d // (rows // tm)) % V7X_LANES == 0

    def run_ffn(x, which, i):
        last = which == 1 and i == depth - 1
        nxt = None if (last or not can_chain) else ffn_stacks[1 - which]
        nxt_layer = i + which
        if ffn_ready[0] is None:
            weights, layer, tf_k = ffn_stacks[which], i, tf
        else:
            weights, layer, tf_k = ffn_ready[0], None, tf_wide
        x, ffn_ready[0] = _ffn(x, ffn_norms[which][i][None], weights, layer, nxt, nxt_layer, tm=tm, tf=tf_k)
        return x

    kv_p = kv_s = st_p = st_s = None
    lp_l, ls_l = [], []
    for i in range(depth):
        x = run_ffn(x, 0, i)
        j = i // 2
        g_mix = norm_mix[i][None]
        if i % 2 == 0:
            proj = _norm_matmul(x, g_mix, fox_w_in_t, j, n=3 * d, tm=tm, tn=tn, out_dtype=F32,
                                w_transposed=True)
            logf = _forget_gate(x, g_mix, fox_w_in_t, j, fox_bias[j][None], row0=3 * d, heads=fox_heads, tm=tm)
            c_col = _prompt_cumsum(logf, batch=batch, seq=seq)
            attn = _fox_prompt(proj, c_col, rows=rows, batch=batch, seq=seq, heads=fox_heads, dh=dh, tq=tq,
                               hps=math.gcd(fox_heads, FOX_HEADS_PER_STEP))
            attn = _fox_decode(page_table, proj, logf, cache_logf_t, cache_k2, cache_v2, j, attn,
                               n_prompt_rows=n_p, heads=fox_heads, dh=dh, n_new=n_new, n_pg=n_pg)
            x = _matmul_res(attn, fox_w_out, j, x, tm=tm_tall, tn=tn_deep)
            kv_p = _heads_layout(proj, kv_p, j, n_fox, row0=0, nrows=n_p, tr=tr, heads=fox_heads, dh=dh)
            kv_s = _heads_layout(proj, kv_s, j, n_fox, row0=n_p, nrows=n_s, tr=n_s, heads=fox_heads, dh=dh)
            lp, ls = split_rows(logf[:, :fox_heads])
            lp_l.append(lp)
            ls_l.append(ls)
        else:
            proj = _norm_matmul(x, g_mix, ret_w_in, j, n=ret_w_in.shape[2], tm=tm, tn=tn_wide, out_dtype=BF16)
            y, st_p = _retention(proj, cos_p, sin_p, None, j, n_ret, None, st_p, row0=0, batch=batch,
                                 n_chunks=seq // RET_CHUNK, rows=RET_CHUNK, t_real=RET_CHUNK,
                                 heads=ret_heads, dk=dk, dv=dv, hps=math.gcd(ret_heads, RET_HEADS_PER_STEP))
            y, st_s = _retention(proj, cos_s, sin_s, state_ret, j, n_ret, y, st_s, row0=n_p, batch=bd,
                                 n_chunks=1, rows=SAMPLE_ROWS, t_real=n_new,
                                 heads=ret_heads, dk=dk, dv=dv, hps=ret_heads)
            x = _matmul_res(y, ret_w_out, j, x, tm=tm, tn=tn_deep)
        x = run_ffn(x, 1, i)

    g_fin = norm_final[None]
    y_prompt = _final_norm(x, g_fin, row0=0, nrows=n_p, tr=tr).reshape(batch, seq, d)
    y_sample = _final_norm(x, g_fin, row0=n_p, nrows=n_s, tr=n_s).reshape(bd, SAMPLE_ROWS, d)[:, :n_new]

    def heads_out(a, lead, n_tok):
        return a.reshape((n_fox,) + lead + (fox_heads, dh))[:, :, :n_tok]

    return (y_prompt, y_sample,
            heads_out(kv_p[0], (batch, seq), seq), heads_out(kv_p[1], (batch, seq), seq),
            jnp.stack(lp_l),
            heads_out(kv_s[0], (bd, SAMPLE_ROWS), n_new), heads_out(kv_s[1], (bd, SAMPLE_ROWS), n_new),
            jnp.stack(ls_l),
            st_p, st_s)
```

```python
import functools
import math

import jax
import jax.numpy as jnp
from jax import lax
from jax.experimental import pallas as pl
from jax.experimental.pallas import tpu as pltpu

F32 = jnp.float32
BF16 = jnp.bfloat16

NORM_EPS = 1e-6
GN_EPS = 1e-5
ROPE_BASE = 10000.0
NEG_BIG = -1e30
LOG2E = math.log2(math.e)

V7X_VMEM_BYTES = 64 * 1024 * 1024
V7X_LANES = 128
V7X_BF16_SUBLANES = 16
V7X_MXU_DIM = 256
VMEM_LIMIT = (V7X_VMEM_BYTES * 15) // 16

SAMPLE_ROWS = V7X_BF16_SUBLANES
RET_CHUNK = 128
RET_UNROLL = 8
N_DECAY_PIECES = 3
HEAD_GROUPS = 4
DECODE_PAGES_PER_STEP = 8
RET_HEADS_PER_STEP = 2
FOX_HEADS_PER_STEP = 4


def _tile(n, target, align):
    for t in range(min(n, target), 0, -1):
        if n % t == 0 and t % align == 0:
            return t
    raise ValueError(f"no tile for {n} (target {target}, align {align})")


def _params(*sem):
    return pltpu.CompilerParams(dimension_semantics=sem, vmem_limit_bytes=VMEM_LIMIT)


def _rms_norm(x, g):
    ms = jnp.mean(x * x, axis=-1, keepdims=True)
    return (x * lax.rsqrt(ms + NORM_EPS)) * g


def _nt_dot(a, b):
    return lax.dot_general(a, b, (((1,), (1,)), ((), ())), preferred_element_type=F32)


def _ffn_kernel(x_ref, g_ref, wg_ref, wu_ref, wd_ref, *refs, tf, cast_next):
    if cast_next:
        (ng_ref, nu_ref, nd_ref), refs = refs[:3], refs[3:]
        o_ref, og_ref, ou_ref, od_ref, h_ref = refs
        og_ref[...] = ng_ref[...].astype(BF16)
        ou_ref[...] = nu_ref[...].astype(BF16)
        od_ref[...] = nd_ref[...].astype(BF16)
    else:
        o_ref, h_ref = refs

    @pl.when(pl.program_id(1) == 0)
    def _():
        x = x_ref[...]
        h_ref[...] = _rms_norm(x, g_ref[...]).astype(BF16)
        o_ref[...] = x

    wgu = jnp.concatenate([wg_ref[...].astype(BF16), wu_ref[...].astype(BF16)], axis=1)
    ab = jnp.dot(h_ref[...], wgu, preferred_element_type=F32)
    a = ab[:, :tf]
    b = ab[:, tf:]
    p = (a * jax.nn.sigmoid(a)) * (0.5 * b)
    o_ref[...] += jnp.dot(p.astype(BF16), wd_ref[...].astype(BF16), preferred_element_type=F32)


def _ffn(x, g, weights, layer, next_weights, next_layer, *, tm, tf):
    r, d = x.shape
    w_gate, w_up, w_down = weights
    f = w_down.shape[-2]
    ni, nj = r // tm, f // tf
    if layer is None:
        w_specs = [pl.BlockSpec((d, tf), lambda i, j: (0, j)),
                   pl.BlockSpec((d, tf), lambda i, j: (0, j)),
                   pl.BlockSpec((tf, d), lambda i, j: (j, 0))]
    else:
        w_specs = [pl.BlockSpec((None, d, tf), lambda i, j: (layer, 0, j)),
                   pl.BlockSpec((None, d, tf), lambda i, j: (layer, 0, j)),
                   pl.BlockSpec((None, tf, d), lambda i, j: (layer, j, 0))]
    in_specs = [pl.BlockSpec((tm, d), lambda i, j: (i, 0)), pl.BlockSpec((1, d), lambda i, j: (0, 0))] + w_specs
    args = [x, g, w_gate, w_up, w_down]
    out_specs = [pl.BlockSpec((tm, d), lambda i, j: (i, 0))]
    out_shape = [jax.ShapeDtypeStruct((r, d), F32)]
    cast_next = next_weights is not None
    if cast_next:
        assert d % ni == 0 and (d // ni) % V7X_LANES == 0
        dr = d // ni
        in_specs += [pl.BlockSpec((None, dr, tf), lambda i, j: (next_layer, i, j)),
                     pl.BlockSpec((None, dr, tf), lambda i, j: (next_layer, i, j)),
                     pl.BlockSpec((None, tf, dr), lambda i, j: (next_layer, j, i))]
        args += list(next_weights)
        out_specs += [pl.BlockSpec((dr, tf), lambda i, j: (i, j)),
                      pl.BlockSpec((dr, tf), lambda i, j: (i, j)),
                      pl.BlockSpec((tf, dr), lambda i, j: (j, i))]
        out_shape += [jax.ShapeDtypeStruct((d, f), BF16), jax.ShapeDtypeStruct((d, f), BF16),
                      jax.ShapeDtypeStruct((f, d), BF16)]
    res = pl.pallas_call(
        functools.partial(_ffn_kernel, tf=tf, cast_next=cast_next),
        out_shape=out_shape,
        grid=(ni, nj),
        in_specs=in_specs,
        out_specs=out_specs,
        scratch_shapes=[pltpu.VMEM((tm, d), BF16)],
        compiler_params=_params("parallel", "arbitrary"),
        name="ffn",
    )(*args)
    return res[0], (tuple(res[1:]) if cast_next else None)


def _norm_matmul_kernel(x_ref, g_ref, w_ref, o_ref, h_ref, *, w_transposed):
    @pl.when(pl.program_id(1) == 0)
    def _():
        h_ref[...] = _rms_norm(x_ref[...], g_ref[...]).astype(BF16)

    w = w_ref[...].astype(BF16)
    if w_transposed:
        out = _nt_dot(h_ref[...], w)
    else:
        out = jnp.dot(h_ref[...], w, preferred_element_type=F32)
    o_ref[...] = out.astype(o_ref.dtype)


def _norm_matmul(x, g, w, layer, *, n, tm, tn, out_dtype, w_transposed=False):
    r, d = x.shape
    if w_transposed:
        w_spec = pl.BlockSpec((None, tn, d), lambda i, j: (layer, j, 0))
    else:
        w_spec = pl.BlockSpec((None, d, tn), lambda i, j: (layer, 0, j))
    return pl.pallas_call(
        functools.partial(_norm_matmul_kernel, w_transposed=w_transposed),
        out_shape=jax.ShapeDtypeStruct((r, n), out_dtype),
        grid=(r // tm, n // tn),
        in_specs=[
            pl.BlockSpec((tm, d), lambda i, j: (i, 0)),
            pl.BlockSpec((1, d), lambda i, j: (0, 0)),
            w_spec,
        ],
        out_specs=pl.BlockSpec((tm, tn), lambda i, j: (i, j)),
        scratch_shapes=[pltpu.VMEM((tm, d), BF16)],
        compiler_params=_params("parallel", "arbitrary"),
        name="norm_matmul",
    )(x, g, w)


def _forget_gate_kernel(x_ref, g_ref, w_ref, b_ref, o_ref, *, heads):
    h = _rms_norm(x_ref[...], g_ref[...]).astype(BF16)
    row = lax.broadcasted_iota(jnp.int32, w_ref.shape, 0)
    w = jnp.where(row < heads, w_ref[...], 0.0).astype(BF16)
    z = _nt_dot(h, w) + b_ref[...]
    o_ref[...] = -(jnp.maximum(-z, 0.0) + jnp.log1p(jnp.exp(-jnp.abs(z))))


def _forget_gate(x, g, w_t, layer, b_f, *, row0, heads, tm):
    r, d = x.shape
    n = V7X_LANES
    cb = row0 // n
    return pl.pallas_call(
        functools.partial(_forget_gate_kernel, heads=heads),
        out_shape=jax.ShapeDtypeStruct((r, n), F32),
        grid=(r // tm,),
        in_specs=[
            pl.BlockSpec((tm, d), lambda i: (i, 0)),
            pl.BlockSpec((1, d), lambda i: (0, 0)),
            pl.BlockSpec((None, n, d), lambda i: (layer, cb, 0)),
            pl.BlockSpec((1, n), lambda i: (0, 0)),
        ],
        out_specs=pl.BlockSpec((tm, n), lambda i: (i, 0)),
        compiler_params=_params("parallel"),
        name="forget_gate",
    )(x, g, w_t, b_f)


def _matmul_res_kernel(a_ref, w_ref, r_ref, o_ref):
    w = w_ref[...].astype(BF16)
    o_ref[...] = r_ref[...] + jnp.dot(a_ref[...], w, preferred_element_type=F32)


def _matmul_res(a, w, layer, res, *, tm, tn):
    r, k = a.shape
    n = w.shape[2]
    return pl.pallas_call(
        _matmul_res_kernel,
        out_shape=jax.ShapeDtypeStruct((r, n), F32),
        grid=(r // tm, n // tn),
        in_specs=[
            pl.BlockSpec((tm, k), lambda i, j: (i, 0)),
            pl.BlockSpec((None, k, tn), lambda i, j: (layer, 0, j)),
            pl.BlockSpec((tm, tn), lambda i, j: (i, j)),
        ],
        out_specs=pl.BlockSpec((tm, tn), lambda i, j: (i, j)),
        compiler_params=_params("parallel", "arbitrary"),
        name="matmul_res",
    )(a, w, res)


def _final_norm_kernel(x_ref, g_ref, o_ref):
    o_ref[...] = _rms_norm(x_ref[...], g_ref[...])


def _final_norm(x, g, *, row0, nrows, tr):
    d = x.shape[1]
    b0 = row0 // tr
    return pl.pallas_call(
        _final_norm_kernel,
        out_shape=jax.ShapeDtypeStruct((nrows, d), F32),
        grid=(nrows // tr,),
        in_specs=[pl.BlockSpec((tr, d), lambda i: (b0 + i, 0)), pl.BlockSpec((1, d), lambda i: (0, 0))],
        out_specs=pl.BlockSpec((tr, d), lambda i: (i, 0)),
        compiler_params=_params("parallel"),
        name="final_norm",
    )(x, g)


def _heads_layout_kernel(*refs, heads, dh, aliased):
    k_ref, v_ref = refs[:2]
    ko_ref, vo_ref = refs[2 + 2 * aliased:]
    tr = k_ref.shape[0]
    for src, dst in ((k_ref, ko_ref), (v_ref, vo_ref)):
        for h in range(heads):
            dst[pl.ds(h, tr, stride=heads), :] = src[:, h * dh:(h + 1) * dh]


def _heads_layout(proj, prev, layer, n_layers, *, row0, nrows, tr, heads, dh):
    d = heads * dh
    b0 = row0 // tr
    aliased = prev is not None
    shape = jax.ShapeDtypeStruct((n_layers, nrows * heads, dh), F32)
    in_specs = [pl.BlockSpec((tr, d), lambda i: (b0 + i, 1)),
                pl.BlockSpec((tr, d), lambda i: (b0 + i, 2))]
    args = [proj, proj]
    if aliased:
        in_specs += [pl.BlockSpec(memory_space=pl.ANY)] * 2
        args += list(prev)
    out_spec = pl.BlockSpec((None, tr * heads, dh), lambda i: (layer, i, 0))
    return pl.pallas_call(
        functools.partial(_heads_layout_kernel, heads=heads, dh=dh, aliased=aliased),
        out_shape=[shape, shape],
        grid=(nrows // tr,),
        in_specs=in_specs,
        out_specs=[out_spec, out_spec],
        input_output_aliases={2: 0, 3: 1} if aliased else {},
        compiler_params=_params("parallel"),
        name="heads_layout",
    )(*args)


def _split_bf16(x):
    pieces = []
    rem = x
    for _ in range(N_DECAY_PIECES):
        p = rem.astype(BF16)
        pieces.append(p)
        rem = rem - p.astype(F32)
    return pieces


def _tri_ones(n, lower):
    row = lax.broadcasted_iota(jnp.int32, (n, n), 0)
    col = lax.broadcasted_iota(jnp.int32, (n, n), 1)
    keep = (col <= row) if lower else (row <= col)
    return jnp.where(keep, 1.0, 0.0).astype(BF16)


def _prompt_cumsum_kernel(lf_ref, c_ref, *, blk):
    t = lf_ref.shape[0]
    lower = _tri_ones(blk, lower=True)
    carry = jnp.zeros((1, lf_ref.shape[1]), F32)
    for i in range(t // blk):
        local = carry
        for piece in _split_bf16(lf_ref[i * blk:(i + 1) * blk, :]):
            local = local + jnp.dot(lower, piece, preferred_element_type=F32)
        c_ref[i * blk:(i + 1) * blk, :] = local
        carry = local[blk - 1:blk, :]


def _prompt_cumsum(logf, *, batch, seq):
    w = logf.shape[1]
    return pl.pallas_call(
        functools.partial(_prompt_cumsum_kernel, blk=V7X_LANES),
        out_shape=jax.ShapeDtypeStruct((batch * seq, w), F32),
        grid=(batch,),
        in_specs=[pl.BlockSpec((seq, w), lambda b: (b, 0))],
        out_specs=pl.BlockSpec((seq, w), lambda b: (b, 0)),
        compiler_params=_params("parallel"),
        name="prompt_cumsum",
    )(logf)


def _lane_cumsum_block(xt, upper):
    out = None
    for piece in _split_bf16(xt):
        part = jnp.dot(piece, upper, preferred_element_type=F32)
        out = part if out is None else out + part
    return out


def _regroup(c, spread, valid, heads):
    y = None
    for piece in _split_bf16(c):
        part = jnp.dot(piece, spread, preferred_element_type=F32)
        y = part if y is None else y + part
    out = []
    for i in range(c.shape[0] // heads):
        blk = y[i * heads:(i + 1) * heads]
        rows = [jnp.sum(jnp.where(v, blk, 0.0), axis=0, keepdims=True) for v in valid]
        out.append(jnp.concatenate(rows, axis=0))
    return out


def _group_masks(heads, m, head_slot):
    head = lax.broadcasted_iota(jnp.int32, (heads, m), 0)
    col = lax.broadcasted_iota(jnp.int32, (heads, m), 1)
    owns = (head // HEAD_GROUPS) == head_slot(col)
    return [jnp.logical_and(lax.rem(head, HEAD_GROUPS) == grp, owns) for grp in range(HEAD_GROUPS)]


def _past_decay(page_refs, carry, heads, page):
    n_pg = len(page_refs)
    per = heads // HEAD_GROUPS
    nr = n_pg * heads
    m_past = page * per
    t_idx = lax.broadcasted_iota(jnp.int32, (page, m_past), 0)
    c_idx = lax.broadcasted_iota(jnp.int32, (page, m_past), 1)
    spread = jnp.where(c_idx // per == t_idx, 1.0, 0.0).astype(BF16)
    upper = _tri_ones(page, lower=False)

    x = jnp.concatenate([ref[...] for ref in page_refs], axis=0)
    within = _lane_cumsum_block(x, upper)
    totals = jnp.broadcast_to(within[:, page - 1:page], (nr, page))
    r_idx = lax.broadcasted_iota(jnp.int32, (nr, nr), 0)
    q_idx = lax.broadcasted_iota(jnp.int32, (nr, nr), 1)
    earlier_page = jnp.logical_and(lax.rem(r_idx, heads) == lax.rem(q_idx, heads), q_idx < r_idx)
    earlier_page = jnp.where(earlier_page, 1.0, 0.0).astype(BF16)
    offset = None
    for piece in _split_bf16(totals):
        part = jnp.dot(earlier_page, piece, preferred_element_type=F32)
        offset = part if offset is None else offset + part
    local = within + offset + jnp.concatenate([carry] * n_pg, axis=0)
    grouped = _regroup(local, spread, _group_masks(heads, m_past, lambda col: lax.rem(col, per)), heads)
    return grouped, local[nr - heads:, page - 1:page]


def _new_token_decay(new_ref, carry, heads):
    nt = new_ref.shape[0]
    per = heads // HEAD_GROUPS
    xn = jnp.concatenate([new_ref[...], jnp.zeros((V7X_LANES - nt, new_ref.shape[1]), F32)], axis=0)
    new = _lane_cumsum_block(xn.T[:heads, :], _tri_ones(V7X_LANES, lower=False)) + carry
    t_new = lax.broadcasted_iota(jnp.int32, (V7X_LANES, V7X_LANES), 0)
    c_new = lax.broadcasted_iota(jnp.int32, (V7X_LANES, V7X_LANES), 1)
    hit = jnp.logical_and(lax.rem(c_new, nt) == t_new, c_new < per * nt)
    return _regroup(new, jnp.where(hit, 1.0, 0.0).astype(BF16),
                    _group_masks(heads, V7X_LANES, lambda col: col // nt), heads)[0]


def _fox_prompt_kernel(q_ref, k_ref, v_ref, c_ref, o_ref, kaug_ref, vt_ref, acc_ref, *, tq, dh, scale):
    qi = pl.program_id(2)
    hps, nk = vt_ref.shape[0], vt_ref.shape[1]
    lanes = c_ref.shape[1]

    @pl.when(qi == 0)
    def _():
        row = lax.broadcasted_iota(jnp.int32, (lanes, dh), 0)
        col = lax.broadcasted_iota(jnp.int32, (lanes, dh), 1)
        for j in range(nk):
            sl = slice(j * tq, (j + 1) * tq)
            pieces = _split_bf16(c_ref[sl, :] * LOG2E)
            for u in range(hps):
                head = pl.program_id(1) * hps + u
                hc = slice(u * dh, (u + 1) * dh)
                aug = jnp.zeros((tq, dh), F32)
                for i, piece in enumerate(pieces):
                    place = jnp.where(jnp.logical_and(row == head, col == i), 1.0, 0.0).astype(BF16)
                    aug = aug + jnp.dot(piece, place, preferred_element_type=F32)
                kaug_ref[u, j, :, :dh] = k_ref[sl, hc].astype(BF16)
                kaug_ref[u, j, :, dh:] = aug.astype(BF16)
                vt_ref[u, j] = v_ref[sl, hc].T.astype(BF16)

    lane = lax.broadcasted_iota(jnp.int32, (tq, dh), 1)
    minus_ones = jnp.where(lane < N_DECAY_PIECES, -1.0, 0.0).astype(BF16)
    q_aug = [jnp.concatenate([(q_ref[:, u * dh:(u + 1) * dh] * (scale * LOG2E)).astype(BF16), minus_ones],
                             axis=1) for u in range(hps)]

    def absorb(kj, m, l, u, diagonal):
        s = _nt_dot(kaug_ref[u, kj], q_aug[u])
        if diagonal:
            key = lax.broadcasted_iota(jnp.int32, (tq, tq), 0)
            qry = lax.broadcasted_iota(jnp.int32, (tq, tq), 1)
            s = jnp.where(key <= qry, s, NEG_BIG)
        m_new = jnp.maximum(m, jnp.max(s, axis=0, keepdims=True))
        alpha = jnp.exp2(m - m_new)
        p = jnp.exp2(s - m_new)
        l = alpha * l + jnp.sum(p, axis=0, keepdims=True)
        pv = jnp.dot(vt_ref[u, kj], p.astype(BF16), preferred_element_type=F32)
        acc_ref[u] = pv if kj == 0 else alpha * acc_ref[u] + pv
        return m_new, l

    for last in range(nk):
        @pl.when(qi == last)
        def _(last=last):
            for u in range(hps):
                m = jnp.full((1, tq), NEG_BIG, F32)
                l = jnp.zeros((1, tq), F32)
                for kj in range(last + 1):
                    m, l = absorb(kj, m, l, u, kj == last)
                o_ref[:, u * dh:(u + 1) * dh] = (acc_ref[u] / l).T.astype(o_ref.dtype)


def _fox_prompt(proj, c_col, *, rows, batch, seq, heads, dh, tq, hps):
    nq = seq // tq
    hb = heads // hps
    w = hps * dh
    return pl.pallas_call(
        functools.partial(_fox_prompt_kernel, tq=tq, dh=dh, scale=dh ** -0.5),
        out_shape=jax.ShapeDtypeStruct((rows, heads * dh), BF16),
        grid=(batch, hb, nq),
        in_specs=[
            pl.BlockSpec((tq, w), lambda b, h, i: (b * nq + i, h)),
            pl.BlockSpec((seq, w), lambda b, h, i: (b, hb + h)),
            pl.BlockSpec((seq, w), lambda b, h, i: (b, 2 * hb + h)),
            pl.BlockSpec((seq, c_col.shape[1]), lambda b, h, i: (b, 0)),
        ],
        out_specs=pl.BlockSpec((tq, w), lambda b, h, i: (b * nq + i, h)),
        scratch_shapes=[pltpu.VMEM((hps, nq, tq, 2 * dh), BF16),
                        pltpu.VMEM((hps, nq, dh, tq), BF16),
                        pltpu.VMEM((hps, dh, tq), F32)],
        compiler_params=_params("parallel", "parallel", "arbitrary"),
        name="fox_prompt",
    )(proj, proj, proj, c_col)


def _fox_decode_kernel(pt_ref, q_ref, kn_ref, vn_ref, lfn_ref, *refs,
                       n_pg, heads, dh, page, n_new, scale):
    lf_refs = refs[:n_pg]
    k_refs = refs[n_pg:2 * n_pg]
    v_refs = refs[2 * n_pg:3 * n_pg]
    o_ref, m_ref, l_ref, acc_ref, carry_ref = refs[3 * n_pg + 1:]
    step = pl.program_id(1)
    nt = q_ref.shape[0]
    nq = m_ref.shape[1] // (heads // HEAD_GROUPS)
    per = heads // HEAD_GROUPS
    gr = per * nq
    cols = page * per

    @pl.when(step == 0)
    def _():
        m_ref[...] = jnp.full_like(m_ref, NEG_BIG)
        l_ref[...] = jnp.zeros_like(l_ref)
        acc_ref[...] = jnp.zeros_like(acc_ref)
        carry_ref[...] = jnp.zeros_like(carry_ref)

    grouped, carry = _past_decay(lf_refs, carry_ref[...], heads, page)
    carry_ref[...] = carry
    ck = jnp.concatenate(grouped, axis=1)

    def group_rows(ref, grp):
        return ref[pl.ds(grp, cols, stride=HEAD_GROUPS), :].astype(BF16)

    def group_queries(grp):
        parts = [q_ref[:nq, (grp + HEAD_GROUPS * j) * dh:(grp + HEAD_GROUPS * j + 1) * dh] for j in range(per)]
        return (jnp.concatenate(parts, axis=0) * scale).astype(BF16)

    def update(grp, s, v_blocks, width):
        m_old = m_ref[grp]
        m_new = jnp.maximum(m_old, jnp.max(s, axis=-1, keepdims=True))
        alpha = jnp.exp(m_old - m_new)
        p = jnp.exp(s - m_new)
        l_ref[grp] = alpha * l_ref[grp] + jnp.sum(p, axis=-1, keepdims=True)
        pv = None
        for i, vb in enumerate(v_blocks):
            w = p[:, i * width:(i + 1) * width].astype(BF16)
            part = jnp.dot(w, vb, preferred_element_type=F32)
            pv = part if pv is None else pv + part
        acc_ref[grp] = alpha * acc_ref[grp] + pv
        m_ref[grp] = m_new

    row = lax.broadcasted_iota(jnp.int32, (gr, n_pg * cols), 0)
    col = lax.broadcasted_iota(jnp.int32, (gr, n_pg * cols), 1)
    own = (row // nq) == lax.rem(col, per)
    for grp in range(HEAD_GROUPS):
        qg = group_queries(grp)
        s = jnp.concatenate([_nt_dot(qg, group_rows(k_refs[i], grp)) for i in range(n_pg)], axis=1)
        s = jnp.where(own, s - ck[grp:grp + 1, :], NEG_BIG)
        update(grp, s, [group_rows(v_refs[i], grp) for i in range(n_pg)], cols)

    @pl.when(step == pl.num_programs(1) - 1)
    def _():
        row = lax.broadcasted_iota(jnp.int32, (gr, V7X_LANES), 0)
        col = lax.broadcasted_iota(jnp.int32, (gr, V7X_LANES), 1)
        t_col = lax.rem(col, nt)
        visible = jnp.logical_and(jnp.logical_and((row // nq) == (col // nt), t_col <= lax.rem(row, nq)),
                                  jnp.logical_and(t_col < n_new, col < per * nt))
        pad = jnp.zeros((V7X_LANES - per * nt, dh), F32)
        pad_rows = jnp.zeros((nt - nq, dh), F32)
        cn = _new_token_decay(lfn_ref, carry, heads)
        for grp in range(HEAD_GROUPS):
            head_cols = [slice((grp + HEAD_GROUPS * j) * dh, (grp + HEAD_GROUPS * j + 1) * dh) for j in range(per)]
            k_new = jnp.concatenate([kn_ref[:, c] for c in head_cols] + [pad], axis=0).astype(BF16)
            v_new = jnp.concatenate([vn_ref[:, c] for c in head_cols] + [pad], axis=0).astype(BF16)
            s = _nt_dot(group_queries(grp), k_new)
            s = jnp.where(visible, s - cn[grp:grp + 1, :], NEG_BIG)
            update(grp, s, [v_new], V7X_LANES)
            out = acc_ref[grp] / l_ref[grp]
            for j, c in enumerate(head_cols):
                rows_j = out[j * nq:(j + 1) * nq]
                if nt > nq:
                    rows_j = jnp.concatenate([rows_j, pad_rows], axis=0)
                o_ref[:, c] = rows_j.astype(o_ref.dtype)


def _fox_decode(page_table, proj, logf, cache_logf, cache_k, cache_v, layer, attn, *,
                n_prompt_rows, heads, dh, n_new, n_pg):
    bd, n_pages = page_table.shape
    page = cache_logf.shape[3]
    per = heads // HEAD_GROUPS
    assert heads % HEAD_GROUPS == 0 and per * SAMPLE_ROWS <= V7X_LANES and page == V7X_LANES
    f32_sublanes = V7X_BF16_SUBLANES // 2
    nq = -(-n_new // f32_sublanes) * f32_sublanes
    gr = per * nq
    hd = heads * dh
    sblk = n_prompt_rows // SAMPLE_ROWS

    def row_spec(col):
        return pl.BlockSpec((SAMPLE_ROWS, hd), lambda b, g, pt: (sblk + b, col))

    def page_spec(i):
        return pl.BlockSpec((None, None, page * heads, dh),
                            lambda b, g, pt: (layer, pt[b, g * n_pg + i], 0, 0))

    def logf_page_spec(i):
        return pl.BlockSpec((None, None, heads, page),
                            lambda b, g, pt: (layer, pt[b, g * n_pg + i], 0, 0))

    attn_arg = 5 + 3 * n_pg
    grid_spec = pltpu.PrefetchScalarGridSpec(
        num_scalar_prefetch=1,
        grid=(bd, n_pages // n_pg),
        in_specs=[row_spec(0), row_spec(1), row_spec(2),
                  pl.BlockSpec((SAMPLE_ROWS, logf.shape[1]), lambda b, g, pt: (sblk + b, 0))]
        + [logf_page_spec(i) for i in range(n_pg)]
        + [page_spec(i) for i in range(n_pg)] * 2
        + [pl.BlockSpec(memory_space=pl.ANY)],
        out_specs=pl.BlockSpec((SAMPLE_ROWS, hd), lambda b, g, pt: (sblk + b, 0)),
        scratch_shapes=[pltpu.VMEM((HEAD_GROUPS, gr, 1), F32),
                        pltpu.VMEM((HEAD_GROUPS, gr, 1), F32),
                        pltpu.VMEM((HEAD_GROUPS, gr, dh), F32),
                        pltpu.VMEM((heads, 1), F32)],
    )
    return pl.pallas_call(
        functools.partial(_fox_decode_kernel, n_pg=n_pg, heads=heads, dh=dh, page=page,
                          n_new=n_new, scale=dh ** -0.5),
        out_shape=jax.ShapeDtypeStruct(attn.shape, attn.dtype),
        grid_spec=grid_spec,
        input_output_aliases={attn_arg: 0},
        compiler_params=_params("parallel", "arbitrary"),
        name="fox_decode",
    )(page_table, proj, proj, proj, logf,
      *([cache_logf] * n_pg), *([cache_k] * n_pg), *([cache_v] * n_pg), attn)


def _ret_tables(heads, chunk, t_real):
    lg = jnp.log1p(-jnp.exp2(-5.0 - jnp.arange(heads, dtype=F32)))
    n = jnp.arange(chunk, dtype=F32)
    diff = n[:, None] - n[None, :]
    causal = diff >= 0
    dmat = jnp.where(causal[None], jnp.exp(jnp.where(causal, diff, 0.0)[None] * lg[:, None, None]), 0.0)
    cross = jnp.exp((n[None, :] + 1.0) * lg[:, None])
    kdec = jnp.where(n[None, :] < t_real, jnp.exp((t_real - 1.0 - n)[None, :] * lg[:, None]), 0.0)
    full = jnp.broadcast_to(jnp.exp(t_real * lg)[:, None], (heads, chunk))
    dvec = jnp.stack([cross, kdec, full], axis=-1)
    return dmat.astype(F32), dvec.astype(F32)


def _rope_tables(positions, dk):
    half = dk // 2
    inv = ROPE_BASE ** (-jnp.arange(half, dtype=F32) / half)
    ang = positions.astype(F32)[:, None] * inv[None, :]
    return jnp.cos(ang), jnp.sin(ang)


def _ret_kernel(*refs, has_state, chunk, rows, n_chunks, dk, dv, hps):
    if has_state:
        q_ref, k_ref, v_ref, g_ref, cos_ref, sin_ref, dmat_ref, dvec_ref, s0_ref, y_ref, st_ref = refs
        st_ref[...] = s0_ref[...]
    else:
        q_ref, k_ref, v_ref, g_ref, cos_ref, sin_ref, dmat_ref, dvec_ref, y_ref, st_ref = refs
        st_ref[...] = jnp.zeros_like(st_ref)
    half = dk // 2

    def padded(x):
        x = x.astype(F32)
        if rows == chunk:
            return x
        return jnp.concatenate([x, jnp.zeros((chunk - rows, x.shape[1]), F32)], axis=0)

    tables = []
    for u in range(hps):
        dvec = dvec_ref[u]
        tables.append((dmat_ref[u], dvec[:, 0:1], dvec[:, 1:2], dvec[0:1, 2:3]))

    def head_step(c, u):
        sl = pl.ds(pl.multiple_of(c * rows, rows), rows)
        qk_cols = slice(u * dk, (u + 1) * dk)
        v_cols = slice(u * dv, (u + 1) * dv)
        dmat, cross_d, k_d, full_d = tables[u]
        cos = padded(cos_ref[sl, :])
        sin = padded(sin_ref[sl, :])

        def rope(x):
            x1 = x[:, :half]
            x2 = x[:, half:]
            return jnp.concatenate([x1 * cos - x2 * sin, x1 * sin + x2 * cos], axis=1)

        q = rope(padded(q_ref[sl, qk_cols]))
        k = rope(padded(k_ref[sl, qk_cols])) * (dk ** -0.5)
        v = padded(v_ref[sl, v_cols]).astype(BF16)
        qb = q.astype(BF16)
        s = _nt_dot(qb, k.astype(BF16)) * dmat
        state = st_ref[0, u]
        y = jnp.dot(s.astype(BF16), v, preferred_element_type=F32)
        y = y + jnp.dot(qb, state.astype(BF16), preferred_element_type=F32) * cross_d
        kdec_t = (k * k_d).T.astype(BF16)
        st_ref[0, u] = full_d * state + jnp.dot(kdec_t, v, preferred_element_type=F32)

        mu = jnp.mean(y, axis=-1, keepdims=True)
        yc = y - mu
        var = jnp.mean(yc * yc, axis=-1, keepdims=True)
        yn = yc * lax.rsqrt(var + GN_EPS)
        gate = padded(g_ref[sl, v_cols])
        out = yn * (gate * jax.nn.sigmoid(gate))
        y_ref[sl, v_cols] = out[:rows].astype(y_ref.dtype)

    def step(c, carry):
        for u in range(hps):
            head_step(c, u)
        return carry

    lax.fori_loop(0, n_chunks, step, 0, unroll=math.gcd(n_chunks, RET_UNROLL))


def _retention(proj, cos, sin, state0, layer, n_layers, y_prev, st_prev, *, row0, batch, n_chunks,
               rows, t_real, heads, dk, dv, hps):
    chunk = max(rows, RET_CHUNK)
    dmat, dvec = _ret_tables(heads, chunk, t_real)
    seq_rows = n_chunks * rows
    r0 = row0 // seq_rows
    qk_w = heads * dk
    has_state = state0 is not None

    def tok_spec(width, col0):
        cb = col0 // (hps * width)
        return pl.BlockSpec((seq_rows, hps * width), lambda b, h: (r0 + b, cb + h))

    in_specs = [
        tok_spec(dk, 0), tok_spec(dk, qk_w), tok_spec(dv, 2 * qk_w), tok_spec(dv, 2 * qk_w + heads * dv),
        pl.BlockSpec((seq_rows, dk // 2), lambda b, h: (0, 0)),
        pl.BlockSpec((seq_rows, dk // 2), lambda b, h: (0, 0)),
        pl.BlockSpec((hps, chunk, chunk), lambda b, h: (h, 0, 0)),
        pl.BlockSpec((hps, chunk, 3), lambda b, h: (h, 0, 0)),
    ]
    args = [proj, proj, proj, proj, cos, sin, dmat, dvec]
    if has_state:
        in_specs.append(pl.BlockSpec((None, 1, hps, dk, dv), lambda b, h: (layer, b, h, 0, 0)))
        args.append(state0)
    n_used = len(args)
    aliases = {}
    for prev, out_idx in ((y_prev, 0), (st_prev, 1)):
        if prev is not None:
            aliases[len(args)] = out_idx
            in_specs.append(pl.BlockSpec(memory_space=pl.ANY))
            args.append(prev)
    n_in = len(args)

    def kernel(*refs):
        refs = refs[:n_used] + refs[n_in:]
        _ret_kernel(*refs, has_state=has_state, chunk=chunk, rows=rows, n_chunks=n_chunks,
                    dk=dk, dv=dv, hps=hps)

    return pl.pallas_call(
        kernel,
        out_shape=[jax.ShapeDtypeStruct((proj.shape[0], heads * dv), BF16),
                   jax.ShapeDtypeStruct((n_layers, batch, heads, dk, dv), F32)],
        grid=(batch, heads // hps),
        in_specs=in_specs,
        out_specs=[
            pl.BlockSpec((seq_rows, hps * dv), lambda b, h: (r0 + b, h)),
            pl.BlockSpec((None, 1, hps, dk, dv), lambda b, h: (layer, b, h, 0, 0)),
        ],
        input_output_aliases=aliases,
        compiler_params=_params("parallel", "parallel"),
        name="retention",
    )(*args)


def kernel(x_prompt, x_sample, cache_fox_k, cache_fox_v, cache_fox_logf, state_ret, page_table, norm_ffn1, ffn1_w_gate, ffn1_w_up, ffn1_w_down, norm_mix, fox_w_in, fox_b_f, fox_w_out, ret_w_in, ret_w_out, norm_ffn2, ffn2_w_gate, ffn2_w_up, ffn2_w_down, norm_final):
    batch, seq, d = x_prompt.shape
    bd, n_new, _ = x_sample.shape
    depth = norm_ffn1.shape[0]
    f = ffn1_w_gate.shape[2]
    n_fox, pool, page, fox_heads, dh = cache_fox_k.shape
    ret_heads, dk, dv = state_ret.shape[2:]
    n_pages = page_table.shape[1]
    past = n_pages * page
    assert n_new <= SAMPLE_ROWS and fox_heads * dh == d and ret_heads * dk == d

    n_p = batch * seq
    n_s = bd * SAMPLE_ROWS
    rows = n_p + n_s
    tm = _tile(rows, 1100, V7X_BF16_SUBLANES)
    tf = _tile(f, V7X_MXU_DIM, V7X_LANES)
    tf_wide = _tile(f, 2 * V7X_MXU_DIM, V7X_LANES)
    tn = _tile(d, 1024, V7X_LANES)
    tn_deep = _tile(d, 512, V7X_LANES)
    tn_wide = _tile(ret_w_in.shape[2], 1536, V7X_LANES)
    tm_tall = _tile(rows, 2 * tm, tm)
    tq = _tile(seq, 1024, V7X_LANES)
    tr = _tile(n_p, 512, n_s)
    n_pg = _tile(n_pages, DECODE_PAGES_PER_STEP, 1)

    xs = jnp.pad(x_sample, ((0, 0), (0, SAMPLE_ROWS - n_new), (0, 0)))
    x = jnp.concatenate([x_prompt.reshape(n_p, d), xs.reshape(n_s, d)], axis=0)

    cache_k2 = cache_fox_k.reshape(n_fox, pool, page * fox_heads, dh)
    cache_v2 = cache_fox_v.reshape(n_fox, pool, page * fox_heads, dh)
    cache_logf_t = jnp.swapaxes(cache_fox_logf, 2, 3)
    fox_bias = jnp.pad(fox_b_f, ((0, 0), (0, V7X_LANES - fox_heads)))
    fox_w_in_t = jnp.swapaxes(fox_w_in, 1, 2)

    cos_p, sin_p = _rope_tables(jnp.arange(seq), dk)
    cos_s, sin_s = _rope_tables(past + jnp.arange(SAMPLE_ROWS), dk)
    n_ret = depth // 2

    def split_rows(a):
        w = a.shape[1]
        return (a[:n_p].reshape(batch, seq, w),
                a[n_p:].reshape(bd, SAMPLE_ROWS, w)[:, :n_new])

    ffn_stacks = ((ffn1_w_gate, ffn1_w_up, ffn1_w_down), (ffn2_w_gate, ffn2_w_up, ffn2_w_down))
    ffn_norms = (norm_ffn1, norm_ffn2)
    ffn_ready = [None]

    can_chain = d % (rows // tm) == 0 and (d // (rows // tm)) % V7X_LANES == 0

    def run_ffn(x, which, i):
        last = which == 1 and i == depth - 1
        nxt = None if (last or not can_chain) else ffn_stacks[1 - which]
        nxt_layer = i + which
        if ffn_ready[0] is None:
            weights, layer, tf_k = ffn_stacks[which], i, tf
        else:
            weights, layer, tf_k = ffn_ready[0], None, tf_wide
        x, ffn_ready[0] = _ffn(x, ffn_norms[which][i][None], weights, layer, nxt, nxt_layer, tm=tm, tf=tf_k)
        return x

    kv_p = kv_s = st_p = st_s = None
    lp_l, ls_l = [], []
    for i in range(depth):
        x = run_ffn(x, 0, i)
        j = i // 2
        g_mix = norm_mix[i][None]
        if i % 2 == 0:
            proj = _norm_matmul(x, g_mix, fox_w_in_t, j, n=3 * d, tm=tm, tn=tn, out_dtype=F32,
                                w_transposed=True)
            logf = _forget_gate(x, g_mix, fox_w_in_t, j, fox_bias[j][None], row0=3 * d, heads=fox_heads, tm=tm)
            c_col = _prompt_cumsum(logf, batch=batch, seq=seq)
            attn = _fox_prompt(proj, c_col, rows=rows, batch=batch, seq=seq, heads=fox_heads, dh=dh, tq=tq,
                               hps=math.gcd(fox_heads, FOX_HEADS_PER_STEP))
            attn = _fox_decode(page_table, proj, logf, cache_logf_t, cache_k2, cache_v2, j, attn,
                               n_prompt_rows=n_p, heads=fox_heads, dh=dh, n_new=n_new, n_pg=n_pg)
            x = _matmul_res(attn, fox_w_out, j, x, tm=tm_tall, tn=tn_deep)
            kv_p = _heads_layout(proj, kv_p, j, n_fox, row0=0, nrows=n_p, tr=tr, heads=fox_heads, dh=dh)
            kv_s = _heads_layout(proj, kv_s, j, n_fox, row0=n_p, nrows=n_s, tr=n_s, heads=fox_heads, dh=dh)
            lp, ls = split_rows(logf[:, :fox_heads])
            lp_l.append(lp)
            ls_l.append(ls)
        else:
            proj = _norm_matmul(x, g_mix, ret_w_in, j, n=ret_w_in.shape[2], tm=tm, tn=tn_wide, out_dtype=BF16)
            y, st_p = _retention(proj, cos_p, sin_p, None, j, n_ret, None, st_p, row0=0, batch=batch,
                                 n_chunks=seq // RET_CHUNK, rows=RET_CHUNK, t_real=RET_CHUNK,
                                 heads=ret_heads, dk=dk, dv=dv, hps=math.gcd(ret_heads, RET_HEADS_PER_STEP))
            y, st_s = _retention(proj, cos_s, sin_s, state_ret, j, n_ret, y, st_s, row0=n_p, batch=bd,
                                 n_chunks=1, rows=SAMPLE_ROWS, t_real=n_new,
                                 heads=ret_heads, dk=dk, dv=dv, hps=ret_heads)
            x = _matmul_res(y, ret_w_out, j, x, tm=tm, tn=tn_deep)
        x = run_ffn(x, 1, i)

    g_fin = norm_final[None]
    y_prompt = _final_norm(x, g_fin, row0=0, nrows=n_p, tr=tr).reshape(batch, seq, d)
    y_sample = _final_norm(x, g_fin, row0=n_p, nrows=n_s, tr=n_s).reshape(bd, SAMPLE_ROWS, d)[:, :n_new]

    def heads_out(a, lead, n_tok):
        return a.reshape((n_fox,) + lead + (fox_heads, dh))[:, :, :n_tok]

    return (y_prompt, y_sample,
            heads_out(kv_p[0], (batch, seq), seq), heads_out(kv_p[1], (batch, seq), seq),
            jnp.stack(lp_l),
            heads_out(kv_s[0], (bd, SAMPLE_ROWS), n_new), heads_out(kv_s[1], (bd, SAMPLE_ROWS), n_new),
            jnp.stack(ls_l),
            st_p, st_s)
```

```python
import functools
import math

import jax
import jax.numpy as jnp
from jax import lax
from jax.experimental import pallas as pl
from jax.experimental.pallas import tpu as pltpu

F32 = jnp.float32
BF16 = jnp.bfloat16

NORM_EPS = 1e-6
GN_EPS = 1e-5
ROPE_BASE = 10000.0
NEG_BIG = -1e30
LOG2E = math.log2(math.e)

V7X_VMEM_BYTES = 64 * 1024 * 1024
V7X_LANES = 128
V7X_BF16_SUBLANES = 16
V7X_MXU_DIM = 256
VMEM_LIMIT = (V7X_VMEM_BYTES * 15) // 16

SAMPLE_ROWS = V7X_BF16_SUBLANES
RET_CHUNK = 128
RET_UNROLL = 8
N_DECAY_PIECES = 3
HEAD_GROUPS = 4
DECODE_PAGES_PER_STEP = 8
RET_HEADS_PER_STEP = 2
FOX_HEADS_PER_STEP = 4


def _tile(n, target, align):
    for t in range(min(n, target), 0, -1):
        if n % t == 0 and t % align == 0:
            return t
    raise ValueError(f"no tile for {n} (target {target}, align {align})")


def _params(*sem):
    return pltpu.CompilerParams(dimension_semantics=sem, vmem_limit_bytes=VMEM_LIMIT)


def _rms_norm(x, g):
    ms = jnp.mean(x * x, axis=-1, keepdims=True)
    return (x * lax.rsqrt(ms + NORM_EPS)) * g


def _nt_dot(a, b):
    return lax.dot_general(a, b, (((1,), (1,)), ((), ())), preferred_element_type=F32)


def _ffn_kernel(x_ref, g_ref, wg_ref, wu_ref, wd_ref, *refs, tf, cast_next):
    if cast_next:
        (ng_ref, nu_ref, nd_ref), refs = refs[:3], refs[3:]
        o_ref, og_ref, ou_ref, od_ref, h_ref = refs
        og_ref[...] = ng_ref[...].astype(BF16)
        ou_ref[...] = nu_ref[...].astype(BF16)
        od_ref[...] = nd_ref[...].astype(BF16)
    else:
        o_ref, h_ref = refs

    @pl.when(pl.program_id(1) == 0)
    def _():
        x = x_ref[...]
        h_ref[...] = _rms_norm(x, g_ref[...]).astype(BF16)
        o_ref[...] = x

    wgu = jnp.concatenate([wg_ref[...].astype(BF16), wu_ref[...].astype(BF16)], axis=1)
    ab = jnp.dot(h_ref[...], wgu, preferred_element_type=F32)
    a = ab[:, :tf]
    b = ab[:, tf:]
    p = (a * jax.nn.sigmoid(a)) * (0.5 * b)
    o_ref[...] += jnp.dot(p.astype(BF16), wd_ref[...].astype(BF16), preferred_element_type=F32)


def _ffn(x, g, weights, layer, next_weights, next_layer, *, tm, tf):
    r, d = x.shape
    w_gate, w_up, w_down = weights
    f = w_down.shape[-2]
    ni, nj = r // tm, f // tf
    if layer is None:
        w_specs = [pl.BlockSpec((d, tf), lambda i, j: (0, j)),
                   pl.BlockSpec((d, tf), lambda i, j: (0, j)),
                   pl.BlockSpec((tf, d), lambda i, j: (j, 0))]
    else:
        w_specs = [pl.BlockSpec((None, d, tf), lambda i, j: (layer, 0, j)),
                   pl.BlockSpec((None, d, tf), lambda i, j: (layer, 0, j)),
                   pl.BlockSpec((None, tf, d), lambda i, j: (layer, j, 0))]
    in_specs = [pl.BlockSpec((tm, d), lambda i, j: (i, 0)), pl.BlockSpec((1, d), lambda i, j: (0, 0))] + w_specs
    args = [x, g, w_gate, w_up, w_down]
    out_specs = [pl.BlockSpec((tm, d), lambda i, j: (i, 0))]
    out_shape = [jax.ShapeDtypeStruct((r, d), F32)]
    cast_next = next_weights is not None
    if cast_next:
        assert d % ni == 0 and (d // ni) % V7X_LANES == 0
        dr = d // ni
        in_specs += [pl.BlockSpec((None, dr, tf), lambda i, j: (next_layer, i, j)),
                     pl.BlockSpec((None, dr, tf), lambda i, j: (next_layer, i, j)),
                     pl.BlockSpec((None, tf, dr), lambda i, j: (next_layer, j, i))]
        args += list(next_weights)
        out_specs += [pl.BlockSpec((dr, tf), lambda i, j: (i, j)),
                      pl.BlockSpec((dr, tf), lambda i, j: (i, j)),
                      pl.BlockSpec((tf, dr), lambda i, j: (j, i))]
        out_shape += [jax.ShapeDtypeStruct((d, f), BF16), jax.ShapeDtypeStruct((d, f), BF16),
                      jax.ShapeDtypeStruct((f, d), BF16)]
    res = pl.pallas_call(
        functools.partial(_ffn_kernel, tf=tf, cast_next=cast_next),
        out_shape=out_shape,
        grid=(ni, nj),
        in_specs=in_specs,
        out_specs=out_specs,
        scratch_shapes=[pltpu.VMEM((tm, d), BF16)],
        compiler_params=_params("parallel", "arbitrary"),
        name="ffn",
    )(*args)
    return res[0], (tuple(res[1:]) if cast_next else None)


def _norm_matmul_kernel(x_ref, g_ref, w_ref, *refs, w_transposed, gate_heads):
    if gate_heads:
        wf_ref, bf_ref, o_ref, lf_ref, h_ref = refs
    else:
        o_ref, h_ref = refs

    @pl.when(pl.program_id(1) == 0)
    def _():
        h = _rms_norm(x_ref[...], g_ref[...]).astype(BF16)
        h_ref[...] = h
        if gate_heads:
            row = lax.broadcasted_iota(jnp.int32, wf_ref.shape, 0)
            wf = jnp.where(row < gate_heads, wf_ref[...], 0.0).astype(BF16)
            z = _nt_dot(h, wf) + bf_ref[...]
            lf_ref[...] = -(jnp.maximum(-z, 0.0) + jnp.log1p(jnp.exp(-jnp.abs(z))))

    w = w_ref[...].astype(BF16)
    if w_transposed:
        out = _nt_dot(h_ref[...], w)
    else:
        out = jnp.dot(h_ref[...], w, preferred_element_type=F32)
    o_ref[...] = out.astype(o_ref.dtype)


def _norm_matmul(x, g, w, layer, *, n, tm, tn, out_dtype, w_transposed=False, gate=None):
    r, d = x.shape
    if w_transposed:
        w_spec = pl.BlockSpec((None, tn, d), lambda i, j: (layer, j, 0))
    else:
        w_spec = pl.BlockSpec((None, d, tn), lambda i, j: (layer, 0, j))
    in_specs = [pl.BlockSpec((tm, d), lambda i, j: (i, 0)), pl.BlockSpec((1, d), lambda i, j: (0, 0)), w_spec]
    args = [x, g, w]
    out_specs = [pl.BlockSpec((tm, tn), lambda i, j: (i, j))]
    out_shape = [jax.ShapeDtypeStruct((r, n), out_dtype)]
    gate_heads = 0
    if gate is not None:
        assert w_transposed
        b_f, row0, gate_heads = gate
        cb = row0 // V7X_LANES
        in_specs += [pl.BlockSpec((None, V7X_LANES, d), lambda i, j: (layer, cb, 0)),
                     pl.BlockSpec((1, V7X_LANES), lambda i, j: (0, 0))]
        args += [w, b_f]
        out_specs.append(pl.BlockSpec((tm, V7X_LANES), lambda i, j: (i, 0)))
        out_shape.append(jax.ShapeDtypeStruct((r, V7X_LANES), F32))
    res = pl.pallas_call(
        functools.partial(_norm_matmul_kernel, w_transposed=w_transposed, gate_heads=gate_heads),
        out_shape=out_shape,
        grid=(r // tm, n // tn),
        in_specs=in_specs,
        out_specs=out_specs,
        scratch_shapes=[pltpu.VMEM((tm, d), BF16)],
        compiler_params=_params("parallel", "arbitrary"),
        name="norm_matmul",
    )(*args)
    return res if gate is not None else res[0]


def _matmul_res_kernel(a_ref, w_ref, r_ref, o_ref):
    w = w_ref[...].astype(BF16)
    o_ref[...] = r_ref[...] + jnp.dot(a_ref[...], w, preferred_element_type=F32)


def _matmul_res(a, w, layer, res, *, tm, tn):
    r, k = a.shape
    n = w.shape[2]
    return pl.pallas_call(
        _matmul_res_kernel,
        out_shape=jax.ShapeDtypeStruct((r, n), F32),
        grid=(r // tm, n // tn),
        in_specs=[
            pl.BlockSpec((tm, k), lambda i, j: (i, 0)),
            pl.BlockSpec((None, k, tn), lambda i, j: (layer, 0, j)),
            pl.BlockSpec((tm, tn), lambda i, j: (i, j)),
        ],
        out_specs=pl.BlockSpec((tm, tn), lambda i, j: (i, j)),
        compiler_params=_params("parallel", "arbitrary"),
        name="matmul_res",
    )(a, w, res)


def _final_norm_kernel(x_ref, g_ref, o_ref):
    o_ref[...] = _rms_norm(x_ref[...], g_ref[...])


def _final_norm(x, g, *, row0, nrows, tr):
    d = x.shape[1]
    b0 = row0 // tr
    return pl.pallas_call(
        _final_norm_kernel,
        out_shape=jax.ShapeDtypeStruct((nrows, d), F32),
        grid=(nrows // tr,),
        in_specs=[pl.BlockSpec((tr, d), lambda i: (b0 + i, 0)), pl.BlockSpec((1, d), lambda i: (0, 0))],
        out_specs=pl.BlockSpec((tr, d), lambda i: (i, 0)),
        compiler_params=_params("parallel"),
        name="final_norm",
    )(x, g)


def _heads_layout_kernel(*refs, heads, dh, aliased):
    k_ref, v_ref = refs[:2]
    ko_ref, vo_ref = refs[2 + 2 * aliased:]
    tr = k_ref.shape[0]
    for src, dst in ((k_ref, ko_ref), (v_ref, vo_ref)):
        for h in range(heads):
            dst[pl.ds(h, tr, stride=heads), :] = src[:, h * dh:(h + 1) * dh]


def _heads_layout(proj, prev, layer, n_layers, *, row0, nrows, tr, heads, dh):
    d = heads * dh
    b0 = row0 // tr
    aliased = prev is not None
    shape = jax.ShapeDtypeStruct((n_layers, nrows * heads, dh), F32)
    in_specs = [pl.BlockSpec((tr, d), lambda i: (b0 + i, 1)),
                pl.BlockSpec((tr, d), lambda i: (b0 + i, 2))]
    args = [proj, proj]
    if aliased:
        in_specs += [pl.BlockSpec(memory_space=pl.ANY)] * 2
        args += list(prev)
    out_spec = pl.BlockSpec((None, tr * heads, dh), lambda i: (layer, i, 0))
    return pl.pallas_call(
        functools.partial(_heads_layout_kernel, heads=heads, dh=dh, aliased=aliased),
        out_shape=[shape, shape],
        grid=(nrows // tr,),
        in_specs=in_specs,
        out_specs=[out_spec, out_spec],
        input_output_aliases={2: 0, 3: 1} if aliased else {},
        compiler_params=_params("parallel"),
        name="heads_layout",
    )(*args)


def _split_bf16(x):
    pieces = []
    rem = x
    for _ in range(N_DECAY_PIECES):
        p = rem.astype(BF16)
        pieces.append(p)
        rem = rem - p.astype(F32)
    return pieces


def _tri_ones(n, lower):
    row = lax.broadcasted_iota(jnp.int32, (n, n), 0)
    col = lax.broadcasted_iota(jnp.int32, (n, n), 1)
    keep = (col <= row) if lower else (row <= col)
    return jnp.where(keep, 1.0, 0.0).astype(BF16)


def _prompt_cumsum_kernel(lf_ref, c_ref, *, blk):
    t = lf_ref.shape[0]
    lower = _tri_ones(blk, lower=True)
    carry = jnp.zeros((1, lf_ref.shape[1]), F32)
    for i in range(t // blk):
        local = carry
        for piece in _split_bf16(lf_ref[i * blk:(i + 1) * blk, :]):
            local = local + jnp.dot(lower, piece, preferred_element_type=F32)
        c_ref[i * blk:(i + 1) * blk, :] = local
        carry = local[blk - 1:blk, :]


def _prompt_cumsum(logf, *, batch, seq):
    w = logf.shape[1]
    return pl.pallas_call(
        functools.partial(_prompt_cumsum_kernel, blk=V7X_LANES),
        out_shape=jax.ShapeDtypeStruct((batch * seq, w), F32),
        grid=(batch,),
        in_specs=[pl.BlockSpec((seq, w), lambda b: (b, 0))],
        out_specs=pl.BlockSpec((seq, w), lambda b: (b, 0)),
        compiler_params=_params("parallel"),
        name="prompt_cumsum",
    )(logf)


def _lane_cumsum_block(xt, upper):
    out = None
    for piece in _split_bf16(xt):
        part = jnp.dot(piece, upper, preferred_element_type=F32)
        out = part if out is None else out + part
    return out


def _regroup(c, spread, valid, heads):
    y = None
    for piece in _split_bf16(c):
        part = jnp.dot(piece, spread, preferred_element_type=F32)
        y = part if y is None else y + part
    out = []
    for i in range(c.shape[0] // heads):
        blk = y[i * heads:(i + 1) * heads]
        rows = [jnp.sum(jnp.where(v, blk, 0.0), axis=0, keepdims=True) for v in valid]
        out.append(jnp.concatenate(rows, axis=0))
    return out


def _group_masks(heads, m, head_slot):
    head = lax.broadcasted_iota(jnp.int32, (heads, m), 0)
    col = lax.broadcasted_iota(jnp.int32, (heads, m), 1)
    owns = (head // HEAD_GROUPS) == head_slot(col)
    return [jnp.logical_and(lax.rem(head, HEAD_GROUPS) == grp, owns) for grp in range(HEAD_GROUPS)]


def _past_decay(page_refs, carry, heads, page):
    n_pg = len(page_refs)
    per = heads // HEAD_GROUPS
    nr = n_pg * heads
    m_past = page * per
    t_idx = lax.broadcasted_iota(jnp.int32, (page, m_past), 0)
    c_idx = lax.broadcasted_iota(jnp.int32, (page, m_past), 1)
    spread = jnp.where(c_idx // per == t_idx, 1.0, 0.0).astype(BF16)
    upper = _tri_ones(page, lower=False)

    x = jnp.concatenate([ref[...] for ref in page_refs], axis=0)
    within = _lane_cumsum_block(x, upper)
    totals = jnp.broadcast_to(within[:, page - 1:page], (nr, page))
    r_idx = lax.broadcasted_iota(jnp.int32, (nr, nr), 0)
    q_idx = lax.broadcasted_iota(jnp.int32, (nr, nr), 1)
    earlier_page = jnp.logical_and(lax.rem(r_idx, heads) == lax.rem(q_idx, heads), q_idx < r_idx)
    earlier_page = jnp.where(earlier_page, 1.0, 0.0).astype(BF16)
    offset = None
    for piece in _split_bf16(totals):
        part = jnp.dot(earlier_page, piece, preferred_element_type=F32)
        offset = part if offset is None else offset + part
    local = within + offset + jnp.concatenate([carry] * n_pg, axis=0)
    grouped = _regroup(local, spread, _group_masks(heads, m_past, lambda col: lax.rem(col, per)), heads)
    return grouped, local[nr - heads:, page - 1:page]


def _new_token_decay(new_ref, carry, heads):
    nt = new_ref.shape[0]
    per = heads // HEAD_GROUPS
    xn = jnp.concatenate([new_ref[...], jnp.zeros((V7X_LANES - nt, new_ref.shape[1]), F32)], axis=0)
    new = _lane_cumsum_block(xn.T[:heads, :], _tri_ones(V7X_LANES, lower=False)) + carry
    t_new = lax.broadcasted_iota(jnp.int32, (V7X_LANES, V7X_LANES), 0)
    c_new = lax.broadcasted_iota(jnp.int32, (V7X_LANES, V7X_LANES), 1)
    hit = jnp.logical_and(lax.rem(c_new, nt) == t_new, c_new < per * nt)
    return _regroup(new, jnp.where(hit, 1.0, 0.0).astype(BF16),
                    _group_masks(heads, V7X_LANES, lambda col: col // nt), heads)[0]


def _fox_prompt_kernel(q_ref, k_ref, v_ref, c_ref, o_ref, kaug_ref, vt_ref, acc_ref, *, tq, dh, scale):
    qi = pl.program_id(2)
    hps, nk = vt_ref.shape[0], vt_ref.shape[1]
    lanes = c_ref.shape[1]

    @pl.when(qi == 0)
    def _():
        row = lax.broadcasted_iota(jnp.int32, (lanes, dh), 0)
        col = lax.broadcasted_iota(jnp.int32, (lanes, dh), 1)
        for j in range(nk):
            sl = slice(j * tq, (j + 1) * tq)
            pieces = _split_bf16(c_ref[sl, :] * LOG2E)
            for u in range(hps):
                head = pl.program_id(1) * hps + u
                hc = slice(u * dh, (u + 1) * dh)
                aug = jnp.zeros((tq, dh), F32)
                for i, piece in enumerate(pieces):
                    place = jnp.where(jnp.logical_and(row == head, col == i), 1.0, 0.0).astype(BF16)
                    aug = aug + jnp.dot(piece, place, preferred_element_type=F32)
                kaug_ref[u, j, :, :dh] = k_ref[sl, hc].astype(BF16)
                kaug_ref[u, j, :, dh:] = aug.astype(BF16)
                vt_ref[u, j] = v_ref[sl, hc].T.astype(BF16)

    lane = lax.broadcasted_iota(jnp.int32, (tq, dh), 1)
    minus_ones = jnp.where(lane < N_DECAY_PIECES, -1.0, 0.0).astype(BF16)
    q_aug = [jnp.concatenate([(q_ref[:, u * dh:(u + 1) * dh] * (scale * LOG2E)).astype(BF16), minus_ones],
                             axis=1) for u in range(hps)]

    def absorb(kj, m, l, u, diagonal):
        s = _nt_dot(kaug_ref[u, kj], q_aug[u])
        if diagonal:
            key = lax.broadcasted_iota(jnp.int32, (tq, tq), 0)
            qry = lax.broadcasted_iota(jnp.int32, (tq, tq), 1)
            s = jnp.where(key <= qry, s, NEG_BIG)
        m_new = jnp.maximum(m, jnp.max(s, axis=0, keepdims=True))
        alpha = jnp.exp2(m - m_new)
        p = jnp.exp2(s - m_new)
        l = alpha * l + jnp.sum(p, axis=0, keepdims=True)
        pv = jnp.dot(vt_ref[u, kj], p.astype(BF16), preferred_element_type=F32)
        acc_ref[u] = pv if kj == 0 else alpha * acc_ref[u] + pv
        return m_new, l

    for last in range(nk):
        @pl.when(qi == last)
        def _(last=last):
            for u in range(hps):
                m = jnp.full((1, tq), NEG_BIG, F32)
                l = jnp.zeros((1, tq), F32)
                for kj in range(last + 1):
                    m, l = absorb(kj, m, l, u, kj == last)
                o_ref[:, u * dh:(u + 1) * dh] = (acc_ref[u] / l).T.astype(o_ref.dtype)


def _fox_prompt(proj, c_col, *, rows, batch, seq, heads, dh, tq, hps):
    nq = seq // tq
    hb = heads // hps
    w = hps * dh
    return pl.pallas_call(
        functools.partial(_fox_prompt_kernel, tq=tq, dh=dh, scale=dh ** -0.5),
        out_shape=jax.ShapeDtypeStruct((rows, heads * dh), BF16),
        grid=(batch, hb, nq),
        in_specs=[
            pl.BlockSpec((tq, w), lambda b, h, i: (b * nq + i, h)),
            pl.BlockSpec((seq, w), lambda b, h, i: (b, hb + h)),
            pl.BlockSpec((seq, w), lambda b, h, i: (b, 2 * hb + h)),
            pl.BlockSpec((seq, c_col.shape[1]), lambda b, h, i: (b, 0)),
        ],
        out_specs=pl.BlockSpec((tq, w), lambda b, h, i: (b * nq + i, h)),
        scratch_shapes=[pltpu.VMEM((hps, nq, tq, 2 * dh), BF16),
                        pltpu.VMEM((hps, nq, dh, tq), BF16),
                        pltpu.VMEM((hps, dh, tq), F32)],
        compiler_params=_params("parallel", "parallel", "arbitrary"),
        name="fox_prompt",
    )(proj, proj, proj, c_col)


def _fox_decode_kernel(pt_ref, q_ref, kn_ref, vn_ref, lfn_ref, *refs,
                       n_pg, heads, dh, page, n_new, scale):
    lf_refs = refs[:n_pg]
    k_refs = refs[n_pg:2 * n_pg]
    v_refs = refs[2 * n_pg:3 * n_pg]
    o_ref, m_ref, l_ref, acc_ref, carry_ref = refs[3 * n_pg + 1:]
    step = pl.program_id(1)
    nt = q_ref.shape[0]
    nq = m_ref.shape[1] // (heads // HEAD_GROUPS)
    per = heads // HEAD_GROUPS
    gr = per * nq
    cols = page * per

    @pl.when(step == 0)
    def _():
        m_ref[...] = jnp.full_like(m_ref, NEG_BIG)
        l_ref[...] = jnp.zeros_like(l_ref)
        acc_ref[...] = jnp.zeros_like(acc_ref)
        carry_ref[...] = jnp.zeros_like(carry_ref)

    grouped, carry = _past_decay(lf_refs, carry_ref[...], heads, page)
    carry_ref[...] = carry
    ck = jnp.concatenate(grouped, axis=1)

    def group_rows(ref, grp):
        return ref[pl.ds(grp, cols, stride=HEAD_GROUPS), :].astype(BF16)

    def group_queries(grp):
        parts = [q_ref[:nq, (grp + HEAD_GROUPS * j) * dh:(grp + HEAD_GROUPS * j + 1) * dh] for j in range(per)]
        return (jnp.concatenate(parts, axis=0) * scale).astype(BF16)

    def update(grp, s, v_blocks, width):
        m_old = m_ref[grp]
        m_new = jnp.maximum(m_old, jnp.max(s, axis=-1, keepdims=True))
        alpha = jnp.exp(m_old - m_new)
        p = jnp.exp(s - m_new)
        l_ref[grp] = alpha * l_ref[grp] + jnp.sum(p, axis=-1, keepdims=True)
        pv = None
        for i, vb in enumerate(v_blocks):
            w = p[:, i * width:(i + 1) * width].astype(BF16)
            part = jnp.dot(w, vb, preferred_element_type=F32)
            pv = part if pv is None else pv + part
        acc_ref[grp] = alpha * acc_ref[grp] + pv
        m_ref[grp] = m_new

    row = lax.broadcasted_iota(jnp.int32, (gr, n_pg * cols), 0)
    col = lax.broadcasted_iota(jnp.int32, (gr, n_pg * cols), 1)
    own = (row // nq) == lax.rem(col, per)
    for grp in range(HEAD_GROUPS):
        qg = group_queries(grp)
        s = jnp.concatenate([_nt_dot(qg, group_rows(k_refs[i], grp)) for i in range(n_pg)], axis=1)
        s = jnp.where(own, s - ck[grp:grp + 1, :], NEG_BIG)
        update(grp, s, [group_rows(v_refs[i], grp) for i in range(n_pg)], cols)

    @pl.when(step == pl.num_programs(1) - 1)
    def _():
        row = lax.broadcasted_iota(jnp.int32, (gr, V7X_LANES), 0)
        col = lax.broadcasted_iota(jnp.int32, (gr, V7X_LANES), 1)
        t_col = lax.rem(col, nt)
        visible = jnp.logical_and(jnp.logical_and((row // nq) == (col // nt), t_col <= lax.rem(row, nq)),
                                  jnp.logical_and(t_col < n_new, col < per * nt))
        pad = jnp.zeros((V7X_LANES - per * nt, dh), F32)
        pad_rows = jnp.zeros((nt - nq, dh), F32)
        cn = _new_token_decay(lfn_ref, carry, heads)
        for grp in range(HEAD_GROUPS):
            head_cols = [slice((grp + HEAD_GROUPS * j) * dh, (grp + HEAD_GROUPS * j + 1) * dh) for j in range(per)]
            k_new = jnp.concatenate([kn_ref[:, c] for c in head_cols] + [pad], axis=0).astype(BF16)
            v_new = jnp.concatenate([vn_ref[:, c] for c in head_cols] + [pad], axis=0).astype(BF16)
            s = _nt_dot(group_queries(grp), k_new)
            s = jnp.where(visible, s - cn[grp:grp + 1, :], NEG_BIG)
            update(grp, s, [v_new], V7X_LANES)
            out = acc_ref[grp] / l_ref[grp]
            for j, c in enumerate(head_cols):
                rows_j = out[j * nq:(j + 1) * nq]
                if nt > nq:
                    rows_j = jnp.concatenate([rows_j, pad_rows], axis=0)
                o_ref[:, c] = rows_j.astype(o_ref.dtype)


def _fox_decode(page_table, proj, logf, cache_logf, cache_k, cache_v, layer, attn, *,
                n_prompt_rows, heads, dh, n_new, n_pg):
    bd, n_pages = page_table.shape
    page = cache_logf.shape[3]
    per = heads // HEAD_GROUPS
    assert heads % HEAD_GROUPS == 0 and per * SAMPLE_ROWS <= V7X_LANES and page == V7X_LANES
    f32_sublanes = V7X_BF16_SUBLANES // 2
    nq = -(-n_new // f32_sublanes) * f32_sublanes
    gr = per * nq
    hd = heads * dh
    sblk = n_prompt_rows // SAMPLE_ROWS

    def row_spec(col):
        return pl.BlockSpec((SAMPLE_ROWS, hd), lambda b, g, pt: (sblk + b, col))

    def page_spec(i):
        return pl.BlockSpec((None, None, page * heads, dh),
                            lambda b, g, pt: (layer, pt[b, g * n_pg + i], 0, 0))

    def logf_page_spec(i):
        return pl.BlockSpec((None, None, heads, page),
                            lambda b, g, pt: (layer, pt[b, g * n_pg + i], 0, 0))

    attn_arg = 5 + 3 * n_pg
    grid_spec = pltpu.PrefetchScalarGridSpec(
        num_scalar_prefetch=1,
        grid=(bd, n_pages // n_pg),
        in_specs=[row_spec(0), row_spec(1), row_spec(2),
                  pl.BlockSpec((SAMPLE_ROWS, logf.shape[1]), lambda b, g, pt: (sblk + b, 0))]
        + [logf_page_spec(i) for i in range(n_pg)]
        + [page_spec(i) for i in range(n_pg)] * 2
        + [pl.BlockSpec(memory_space=pl.ANY)],
        out_specs=pl.BlockSpec((SAMPLE_ROWS, hd), lambda b, g, pt: (sblk + b, 0)),
        scratch_shapes=[pltpu.VMEM((HEAD_GROUPS, gr, 1), F32),
                        pltpu.VMEM((HEAD_GROUPS, gr, 1), F32),
                        pltpu.VMEM((HEAD_GROUPS, gr, dh), F32),
                        pltpu.VMEM((heads, 1), F32)],
    )
    return pl.pallas_call(
        functools.partial(_fox_decode_kernel, n_pg=n_pg, heads=heads, dh=dh, page=page,
                          n_new=n_new, scale=dh ** -0.5),
        out_shape=jax.ShapeDtypeStruct(attn.shape, attn.dtype),
        grid_spec=grid_spec,
        input_output_aliases={attn_arg: 0},
        compiler_params=_params("parallel", "arbitrary"),
        name="fox_decode",
    )(page_table, proj, proj, proj, logf,
      *([cache_logf] * n_pg), *([cache_k] * n_pg), *([cache_v] * n_pg), attn)


def _ret_tables(heads, chunk, t_real):
    lg = jnp.log1p(-jnp.exp2(-5.0 - jnp.arange(heads, dtype=F32)))
    n = jnp.arange(chunk, dtype=F32)
    diff = n[:, None] - n[None, :]
    causal = diff >= 0
    dmat = jnp.where(causal[None], jnp.exp(jnp.where(causal, diff, 0.0)[None] * lg[:, None, None]), 0.0)
    cross = jnp.exp((n[None, :] + 1.0) * lg[:, None])
    kdec = jnp.where(n[None, :] < t_real, jnp.exp((t_real - 1.0 - n)[None, :] * lg[:, None]), 0.0)
    full = jnp.broadcast_to(jnp.exp(t_real * lg)[:, None], (heads, chunk))
    dvec = jnp.stack([cross, kdec, full], axis=-1)
    return dmat.astype(F32), dvec.astype(F32)


def _rope_tables(positions, dk):
    half = dk // 2
    inv = ROPE_BASE ** (-jnp.arange(half, dtype=F32) / half)
    ang = positions.astype(F32)[:, None] * inv[None, :]
    return jnp.cos(ang), jnp.sin(ang)


def _ret_kernel(*refs, has_state, chunk, rows, n_chunks, dk, dv, hps):
    if has_state:
        q_ref, k_ref, v_ref, g_ref, cos_ref, sin_ref, dmat_ref, dvec_ref, s0_ref, y_ref, st_ref = refs
        st_ref[...] = s0_ref[...]
    else:
        q_ref, k_ref, v_ref, g_ref, cos_ref, sin_ref, dmat_ref, dvec_ref, y_ref, st_ref = refs
        st_ref[...] = jnp.zeros_like(st_ref)
    half = dk // 2

    def padded(x):
        x = x.astype(F32)
        if rows == chunk:
            return x
        return jnp.concatenate([x, jnp.zeros((chunk - rows, x.shape[1]), F32)], axis=0)

    tables = []
    for u in range(hps):
        dvec = dvec_ref[u]
        tables.append((dmat_ref[u], dvec[:, 0:1], dvec[:, 1:2], dvec[0:1, 2:3]))

    def head_step(c, u):
        sl = pl.ds(pl.multiple_of(c * rows, rows), rows)
        qk_cols = slice(u * dk, (u + 1) * dk)
        v_cols = slice(u * dv, (u + 1) * dv)
        dmat, cross_d, k_d, full_d = tables[u]
        cos = padded(cos_ref[sl, :])
        sin = padded(sin_ref[sl, :])

        def rope(x):
            x1 = x[:, :half]
            x2 = x[:, half:]
            return jnp.concatenate([x1 * cos - x2 * sin, x1 * sin + x2 * cos], axis=1)

        q = rope(padded(q_ref[sl, qk_cols]))
        k = rope(padded(k_ref[sl, qk_cols])) * (dk ** -0.5)
        v = padded(v_ref[sl, v_cols]).astype(BF16)
        qb = q.astype(BF16)
        s = _nt_dot(qb, k.astype(BF16)) * dmat
        state = st_ref[0, u]
        y = jnp.dot(s.astype(BF16), v, preferred_element_type=F32)
        y = y + jnp.dot(qb, state.astype(BF16), preferred_element_type=F32) * cross_d
        kdec_t = (k * k_d).T.astype(BF16)
        st_ref[0, u] = full_d * state + jnp.dot(kdec_t, v, preferred_element_type=F32)

        mu = jnp.mean(y, axis=-1, keepdims=True)
        yc = y - mu
        var = jnp.mean(yc * yc, axis=-1, keepdims=True)
        yn = yc * lax.rsqrt(var + GN_EPS)
        gate = padded(g_ref[sl, v_cols])
        out = yn * (gate * jax.nn.sigmoid(gate))
        y_ref[sl, v_cols] = out[:rows].astype(y_ref.dtype)

    def step(c, carry):
        for u in range(hps):
            head_step(c, u)
        return carry

    lax.fori_loop(0, n_chunks, step, 0, unroll=math.gcd(n_chunks, RET_UNROLL))


def _retention(proj, cos, sin, state0, layer, n_layers, y_prev, st_prev, *, row0, batch, n_chunks,
               rows, t_real, heads, dk, dv, hps):
    chunk = max(rows, RET_CHUNK)
    dmat, dvec = _ret_tables(heads, chunk, t_real)
    seq_rows = n_chunks * rows
    r0 = row0 // seq_rows
    qk_w = heads * dk
    has_state = state0 is not None

    def tok_spec(width, col0):
        cb = col0 // (hps * width)
        return pl.BlockSpec((seq_rows, hps * width), lambda b, h: (r0 + b, cb + h))

    in_specs = [
        tok_spec(dk, 0), tok_spec(dk, qk_w), tok_spec(dv, 2 * qk_w), tok_spec(dv, 2 * qk_w + heads * dv),
        pl.BlockSpec((seq_rows, dk // 2), lambda b, h: (0, 0)),
        pl.BlockSpec((seq_rows, dk // 2), lambda b, h: (0, 0)),
        pl.BlockSpec((hps, chunk, chunk), lambda b, h: (h, 0, 0)),
        pl.BlockSpec((hps, chunk, 3), lambda b, h: (h, 0, 0)),
    ]
    args = [proj, proj, proj, proj, cos, sin, dmat, dvec]
    if has_state:
        in_specs.append(pl.BlockSpec((None, 1, hps, dk, dv), lambda b, h: (layer, b, h, 0, 0)))
        args.append(state0)
    n_used = len(args)
    aliases = {}
    for prev, out_idx in ((y_prev, 0), (st_prev, 1)):
        if prev is not None:
            aliases[len(args)] = out_idx
            in_specs.append(pl.BlockSpec(memory_space=pl.ANY))
            args.append(prev)
    n_in = len(args)

    def kernel(*refs):
        refs = refs[:n_used] + refs[n_in:]
        _ret_kernel(*refs, has_state=has_state, chunk=chunk, rows=rows, n_chunks=n_chunks,
                    dk=dk, dv=dv, hps=hps)

    return pl.pallas_call(
        kernel,
        out_shape=[jax.ShapeDtypeStruct((proj.shape[0], heads * dv), BF16),
                   jax.ShapeDtypeStruct((n_layers, batch, heads, dk, dv), F32)],
        grid=(batch, heads // hps),
        in_specs=in_specs,
        out_specs=[
            pl.BlockSpec((seq_rows, hps * dv), lambda b, h: (r0 + b, h)),
            pl.BlockSpec((None, 1, hps, dk, dv), lambda b, h: (layer, b, h, 0, 0)),
        ],
        input_output_aliases=aliases,
        compiler_params=_params("parallel", "parallel"),
        name="retention",
    )(*args)


def kernel(x_prompt, x_sample, cache_fox_k, cache_fox_v, cache_fox_logf, state_ret, page_table, norm_ffn1, ffn1_w_gate, ffn1_w_up, ffn1_w_down, norm_mix, fox_w_in, fox_b_f, fox_w_out, ret_w_in, ret_w_out, norm_ffn2, ffn2_w_gate, ffn2_w_up, ffn2_w_down, norm_final):
    batch, seq, d = x_prompt.shape
    bd, n_new, _ = x_sample.shape
    depth = norm_ffn1.shape[0]
    f = ffn1_w_gate.shape[2]
    n_fox, pool, page, fox_heads, dh = cache_fox_k.shape
    ret_heads, dk, dv = state_ret.shape[2:]
    n_pages = page_table.shape[1]
    past = n_pages * page
    assert n_new <= SAMPLE_ROWS and fox_heads * dh == d and ret_heads * dk == d

    n_p = batch * seq
    n_s = bd * SAMPLE_ROWS
    rows = n_p + n_s
    tm = _tile(rows, 1100, V7X_BF16_SUBLANES)
    tf = _tile(f, V7X_MXU_DIM, V7X_LANES)
    tf_wide = _tile(f, 2 * V7X_MXU_DIM, V7X_LANES)
    tn = _tile(d, 1024, V7X_LANES)
    tn_deep = _tile(d, 512, V7X_LANES)
    tn_wide = _tile(ret_w_in.shape[2], 1536, V7X_LANES)
    tm_tall = _tile(rows, 2 * tm, tm)
    tq = _tile(seq, 1024, V7X_LANES)
    tr = _tile(n_p, 512, n_s)
    n_pg = _tile(n_pages, DECODE_PAGES_PER_STEP, 1)

    xs = jnp.pad(x_sample, ((0, 0), (0, SAMPLE_ROWS - n_new), (0, 0)))
    x = jnp.concatenate([x_prompt.reshape(n_p, d), xs.reshape(n_s, d)], axis=0)

    cache_k2 = cache_fox_k.reshape(n_fox, pool, page * fox_heads, dh)
    cache_v2 = cache_fox_v.reshape(n_fox, pool, page * fox_heads, dh)
    cache_logf_t = jnp.swapaxes(cache_fox_logf, 2, 3)
    fox_bias = jnp.pad(fox_b_f, ((0, 0), (0, V7X_LANES - fox_heads)))
    fox_w_in_t = jnp.swapaxes(fox_w_in, 1, 2)

    cos_p, sin_p = _rope_tables(jnp.arange(seq), dk)
    cos_s, sin_s = _rope_tables(past + jnp.arange(SAMPLE_ROWS), dk)
    n_ret = depth // 2

    def split_rows(a):
        w = a.shape[1]
        return (a[:n_p].reshape(batch, seq, w),
                a[n_p:].reshape(bd, SAMPLE_ROWS, w)[:, :n_new])

    ffn_stacks = ((ffn1_w_gate, ffn1_w_up, ffn1_w_down), (ffn2_w_gate, ffn2_w_up, ffn2_w_down))
    ffn_norms = (norm_ffn1, norm_ffn2)
    ffn_ready = [None]

    can_chain = d % (rows // tm) == 0 and (d // (rows // tm)) % V7X_LANES == 0

    def run_ffn(x, which, i):
        last = which == 1 and i == depth - 1
        nxt = None if (last or not can_chain) else ffn_stacks[1 - which]
        nxt_layer = i + which
        if ffn_ready[0] is None:
            weights, layer, tf_k = ffn_stacks[which], i, tf
        else:
            weights, layer, tf_k = ffn_ready[0], None, tf_wide
        x, ffn_ready[0] = _ffn(x, ffn_norms[which][i][None], weights, layer, nxt, nxt_layer, tm=tm, tf=tf_k)
        return x

    kv_p = kv_s = st_p = st_s = None
    lp_l, ls_l = [], []
    for i in range(depth):
        x = run_ffn(x, 0, i)
        j = i // 2
        g_mix = norm_mix[i][None]
        if i % 2 == 0:
            proj, logf = _norm_matmul(x, g_mix, fox_w_in_t, j, n=3 * d, tm=tm, tn=tn, out_dtype=F32,
                                      w_transposed=True, gate=(fox_bias[j][None], 3 * d, fox_heads))
            c_col = _prompt_cumsum(logf, batch=batch, seq=seq)
            attn = _fox_prompt(proj, c_col, rows=rows, batch=batch, seq=seq, heads=fox_heads, dh=dh, tq=tq,
                               hps=math.gcd(fox_heads, FOX_HEADS_PER_STEP))
            attn = _fox_decode(page_table, proj, logf, cache_logf_t, cache_k2, cache_v2, j, attn,
                               n_prompt_rows=n_p, heads=fox_heads, dh=dh, n_new=n_new, n_pg=n_pg)
            x = _matmul_res(attn, fox_w_out, j, x, tm=tm_tall, tn=tn_deep)
            kv_p = _heads_layout(proj, kv_p, j, n_fox, row0=0, nrows=n_p, tr=tr, heads=fox_heads, dh=dh)
            kv_s = _heads_layout(proj, kv_s, j, n_fox, row0=n_p, nrows=n_s, tr=n_s, heads=fox_heads, dh=dh)
            lp, ls = split_rows(logf[:, :fox_heads])
            lp_l.append(lp)
            ls_l.append(ls)
        else:
            proj = _norm_matmul(x, g_mix, ret_w_in, j, n=ret_w_in.shape[2], tm=tm, tn=tn_wide, out_dtype=BF16)
            y, st_p = _retention(proj, cos_p, sin_p, None, j, n_ret, None, st_p, row0=0, batch=batch,
                                 n_chunks=seq // RET_CHUNK, rows=RET_CHUNK, t_real=RET_CHUNK,
                                 heads=ret_heads, dk=dk, dv=dv, hps=math.gcd(ret_heads, RET_HEADS_PER_STEP))
            y, st_s = _retention(proj, cos_s, sin_s, state_ret, j, n_ret, y, st_s, row0=n_p, batch=bd,
                                 n_chunks=1, rows=SAMPLE_ROWS, t_real=n_new,
                                 heads=ret_heads, dk=dk, dv=dv, hps=ret_heads)
            x = _matmul_res(y, ret_w_out, j, x, tm=tm, tn=tn_deep)
        x = run_ffn(x, 1, i)

    g_fin = norm_final[None]
    y_prompt = _final_norm(x, g_fin, row0=0, nrows=n_p, tr=tr).reshape(batch, seq, d)
    y_sample = _final_norm(x, g_fin, row0=n_p, nrows=n_s, tr=n_s).reshape(bd, SAMPLE_ROWS, d)[:, :n_new]

    def heads_out(a, lead, n_tok):
        return a.reshape((n_fox,) + lead + (fox_heads, dh))[:, :, :n_tok]

    return (y_prompt, y_sample,
            heads_out(kv_p[0], (batch, seq), seq), heads_out(kv_p[1], (batch, seq), seq),
            jnp.stack(lp_l),
            heads_out(kv_s[0], (bd, SAMPLE_ROWS), n_new), heads_out(kv_s[1], (bd, SAMPLE_ROWS), n_new),
            jnp.stack(ls_l),
            st_p, st_s)
```
